```python
import jax, jax.numpy as jnp
from jax import lax
import numpy as np

D_MODEL = 1024
BATCH = 8
SEQ = 8192
DEPTH = 2
DEC_BATCH = 8
DEC_SEQ = 32
PAST_LEN = 2048

CHUNK = 64
N_MIXERS = 2
N_RET_LAYERS = (DEPTH + 1) // 2
N_GLA_LAYERS = DEPTH // 2
RET_HEADS = 4
RET_DK = D_MODEL // RET_HEADS
RET_DV = 2 * D_MODEL // RET_HEADS
RET_QK = RET_HEADS * RET_DK
RET_V = RET_HEADS * RET_DV
ROPE_BASE = 10000.0
GLA_HEADS = 4
GLA_DK = D_MODEL // 2 // GLA_HEADS
GLA_DV = D_MODEL // GLA_HEADS
GLA_QK = GLA_HEADS * GLA_DK
GLA_V = GLA_HEADS * GLA_DV
GLA_GATE_RANK = 16
GLA_TAU = 16.0
D_FF = 2816
CONV_W = 3
EPS = 1e-6

kernel_name = "hybrid_retention_gla_convffn_step"


def rmsnorm(x, g):
    xf = x.astype(jnp.float32)
    y = xf * lax.rsqrt(jnp.mean(xf * xf, axis=-1, keepdims=True) + EPS)
    return (y * g.astype(jnp.float32)).astype(x.dtype)


def rotary(x, pos):
    half = x.shape[-1] // 2
    inv = ROPE_BASE ** (-jnp.arange(half, dtype=jnp.float32) / half)
    ang = pos.astype(jnp.float32)[:, None] * inv[None, :]
    cos = jnp.cos(ang)[None, :, None, :]
    sin = jnp.sin(ang)[None, :, None, :]
    x1, x2 = x[..., :half], x[..., half:]
    return jnp.concatenate([x1 * cos - x2 * sin, x1 * sin + x2 * cos], axis=-1)


def run_chunked(step, S0, seqs):
    T = seqs[0].shape[1]
    if T <= CHUNK:
        return step(S0, seqs)
    n = T // CHUNK
    def to_chunks(a):
        return jnp.moveaxis(a.reshape((a.shape[0], n, CHUNK) + a.shape[2:]), 1, 0)
    S, out = lax.scan(step, S0, tuple(to_chunks(a) for a in seqs))
    out = jnp.moveaxis(out, 0, 1)
    return S, out.reshape((out.shape[0], T) + out.shape[3:])


def retention_chunk(S, xs, log_gamma):
    q, k, v = xs
    L = q.shape[1]
    idx = jnp.arange(L, dtype=jnp.float32)
    diff = idx[:, None] - idx[None, :]
    lg = log_gamma[:, None, None]
    decay = jnp.where(diff[None] >= 0.0, jnp.exp(lg * jnp.maximum(diff, 0.0)[None]), 0.0)
    scores = jnp.einsum('bihd,bjhd->bhij', q, k) * decay[None]
    intra = jnp.einsum('bhij,bjhe->bihe', scores, v)
    q_dec = jnp.exp(log_gamma[:, None] * (idx[None, :] + 1.0))
    cross = jnp.einsum('bihd,hi,bhde->bihe', q, q_dec, S)
    k_dec = jnp.exp(log_gamma[:, None] * (L - 1.0 - idx[None, :]))
    S_new = jnp.exp(log_gamma * L)[None, :, None, None] * S + jnp.einsum('bjhd,hj,bjhe->bhde', k, k_dec, v)
    return S_new, intra + cross


def retention_mixer(h, S0, pos, w_in, gn_g, w_out):
    B, T, _ = h.shape
    proj = h @ w_in
    q, k, v, g = jnp.split(proj, [RET_QK, 2 * RET_QK, 2 * RET_QK + RET_V], axis=-1)
    q = rotary(q.astype(jnp.float32).reshape(B, T, RET_HEADS, RET_DK), pos)
    k = rotary(k.astype(jnp.float32).reshape(B, T, RET_HEADS, RET_DK), pos) * (RET_DK ** -0.5)
    v = v.astype(jnp.float32).reshape(B, T, RET_HEADS, RET_DV)
    log_gamma = jnp.log1p(-jnp.exp2(-5.0 - jnp.arange(RET_HEADS, dtype=jnp.float32)))
    step = lambda S, xs: retention_chunk(S, xs, log_gamma)
    S, o = run_chunked(step, S0.astype(jnp.float32), (q, k, v))
    mu = jnp.mean(o, axis=-1, keepdims=True)
    var = jnp.mean(jnp.square(o - mu), axis=-1, keepdims=True)
    o = (o - mu) * lax.rsqrt(var + EPS) * gn_g.astype(jnp.float32)[None, None]
    o = o.reshape(B, T, RET_V).astype(h.dtype)
    return (jax.nn.silu(g) * o) @ w_out, S


def gla_chunk(S, xs):
    q, k, v, lg = xs
    L = q.shape[1]
    b = jnp.cumsum(lg, axis=1)
    causal = jnp.tril(jnp.ones((L, L), dtype=bool))
    expo = b[:, :, None] - b[:, None, :]
    expo = jnp.where(causal[None, :, :, None, None], expo, -jnp.inf)
    A = jnp.einsum('bihd,bjhd,bijhd->bhij', q, k, jnp.exp(expo))
    intra = jnp.einsum('bhij,bjhe->bihe', A, v)
    cross = jnp.einsum('bihd,bhde->bihe', q * jnp.exp(b), S)
    bL = b[:, -1]
    S_new = jnp.exp(bL)[..., None] * S + jnp.einsum('bjhd,bjhe->bhde', k * jnp.exp(bL[:, None] - b), v)
    return S_new, intra + cross


def gla_mixer(h, S0, w_in, w_a2, b_a, norm_g, w_out):
    B, T, _ = h.shape
    proj = h @ w_in
    q, k, v, r, a = jnp.split(proj, [GLA_QK, 2 * GLA_QK, 2 * GLA_QK + GLA_V, 2 * GLA_QK + 2 * GLA_V], axis=-1)
    q = q.astype(jnp.float32).reshape(B, T, GLA_HEADS, GLA_DK)
    k = k.astype(jnp.float32).reshape(B, T, GLA_HEADS, GLA_DK) * (GLA_DK ** -0.5)
    v = v.astype(jnp.float32).reshape(B, T, GLA_HEADS, GLA_DV)
    lg = jax.nn.log_sigmoid((a @ w_a2 + b_a).astype(jnp.float32)) / GLA_TAU
    lg = lg.reshape(B, T, GLA_HEADS, GLA_DK)
    S, o = run_chunked(gla_chunk, S0.astype(jnp.float32), (q, k, v, lg))
    o = o * lax.rsqrt(jnp.mean(o * o, axis=-1, keepdims=True) + EPS) * norm_g.astype(jnp.float32)[None, None]
    o = o.reshape(B, T, GLA_V).astype(h.dtype)
    return (jax.nn.silu(r) * o) @ w_out, S


def conv_ffn(h, conv_state, w_up, conv_w, conv_b, w_down):
    u = h @ w_up
    T = u.shape[1]
    ext = jnp.concatenate([conv_state.astype(u.dtype), u], axis=1)
    c = conv_b + ext[:, 0:T] * conv_w[0]
    for j in range(1, CONV_W):
        c = c + ext[:, j:j + T] * conv_w[j]
    gate, val = jnp.split(c, 2, axis=-1)
    return (jax.nn.silu(gate) * val) @ w_down, ext[:, T:]


def trunk(x, pos, ret_states, gla_states, conv_states,
          norm_mix, norm_ffn, norm_final,
          ret_w_in, ret_gn_g, ret_w_out,
          gla_w_in, gla_w_a2, gla_b_a, gla_norm_g, gla_w_out,
          ffn_w_up, ffn_conv_w, ffn_conv_b, ffn_w_down):
    new_ret, new_gla, new_conv = [], [], []
    for i in range(DEPTH):
        h = rmsnorm(x, norm_mix[i])
        j = i // N_MIXERS
        if i % N_MIXERS == 0:
            y, S = retention_mixer(h, ret_states[j], pos, ret_w_in[j], ret_gn_g[j], ret_w_out[j])
            new_ret.append(S)
        else:
            y, S = gla_mixer(h, gla_states[j], gla_w_in[j], gla_w_a2[j], gla_b_a[j], gla_norm_g[j], gla_w_out[j])
            new_gla.append(S)
        x = x + y
        h = rmsnorm(x, norm_ffn[i])
        y, cs = conv_ffn(h, conv_states[i], ffn_w_up[i], ffn_conv_w[i], ffn_conv_b[i], ffn_w_down[i])
        new_conv.append(cs)
        x = x + y
    return rmsnorm(x, norm_final), jnp.stack(new_ret), jnp.stack(new_gla), jnp.stack(new_conv)


def setup_inputs(seed: int = 0) -> dict:
    key = jax.random.key(seed)
    ks = jax.random.split(key, 24)
    nrm = lambda k, shape, s: jax.random.normal(k, shape, dtype=jnp.float32) * s
    D = D_MODEL
    return {
        "x_prompt": nrm(ks[0], (BATCH, SEQ, D), 1.0),
        "x_sample": nrm(ks[1], (DEC_BATCH, DEC_SEQ, D), 1.0),
        "state_ret": nrm(ks[2], (N_RET_LAYERS, DEC_BATCH, RET_HEADS, RET_DK, RET_DV), 0.5),
        "state_gla": nrm(ks[3], (N_GLA_LAYERS, DEC_BATCH, GLA_HEADS, GLA_DK, GLA_DV), 0.5),
        "cache_conv": nrm(ks[4], (DEPTH, DEC_BATCH, CONV_W - 1, 2 * D_FF), 1.0),
        "norm_mix": 1.0 + nrm(ks[5], (DEPTH, D), 0.02),
        "norm_ffn": 1.0 + nrm(ks[6], (DEPTH, D), 0.02),
        "norm_final": 1.0 + nrm(ks[7], (D,), 0.02),
        "ret_w_in": nrm(ks[8], (N_RET_LAYERS, D, 2 * RET_QK + 2 * RET_V), D ** -0.5),
        "ret_gn_g": 1.0 + nrm(ks[9], (N_RET_LAYERS, RET_HEADS, RET_DV), 0.02),
        "ret_w_out": nrm(ks[10], (N_RET_LAYERS, RET_V, D), RET_V ** -0.5),
        "gla_w_in": nrm(ks[11], (N_GLA_LAYERS, D, 2 * GLA_QK + 2 * GLA_V + GLA_GATE_RANK), D ** -0.5),
        "gla_w_a2": nrm(ks[12], (N_GLA_LAYERS, GLA_GATE_RANK, GLA_QK), GLA_GATE_RANK ** -0.5),
        "gla_b_a": nrm(ks[13], (N_GLA_LAYERS, GLA_QK), 0.01),
        "gla_norm_g": 1.0 + nrm(ks[14], (N_GLA_LAYERS, GLA_HEADS, GLA_DV), 0.02),
        "gla_w_out": nrm(ks[15], (N_GLA_LAYERS, GLA_V, D), GLA_V ** -0.5),
        "ffn_w_up": nrm(ks[16], (DEPTH, D, 2 * D_FF), D ** -0.5),
        "ffn_conv_w": nrm(ks[17], (DEPTH, CONV_W, 2 * D_FF), CONV_W ** -0.5),
        "ffn_conv_b": nrm(ks[18], (DEPTH, 2 * D_FF), 0.01),
        "ffn_w_down": nrm(ks[19], (DEPTH, D_FF, D), D_FF ** -0.5),
    }


def reference(x_prompt, x_sample, state_ret, state_gla, cache_conv,
              norm_mix, norm_ffn, norm_final,
              ret_w_in, ret_gn_g, ret_w_out,
              gla_w_in, gla_w_a2, gla_b_a, gla_norm_g, gla_w_out,
              ffn_w_up, ffn_conv_w, ffn_conv_b, ffn_w_down):
    weights = (norm_mix, norm_ffn, norm_final,
               ret_w_in, ret_gn_g, ret_w_out,
               gla_w_in, gla_w_a2, gla_b_a, gla_norm_g, gla_w_out,
               ffn_w_up, ffn_conv_w, ffn_conv_b, ffn_w_down)
    B = x_prompt.shape[0]
    ret0 = jnp.zeros((N_RET_LAYERS, B, RET_HEADS, RET_DK, RET_DV), jnp.float32)
    gla0 = jnp.zeros((N_GLA_LAYERS, B, GLA_HEADS, GLA_DK, GLA_DV), jnp.float32)
    conv0 = jnp.zeros((DEPTH, B, CONV_W - 1, 2 * D_FF), x_prompt.dtype)
    pos_p = jnp.arange(x_prompt.shape[1], dtype=jnp.int32)
    y_prompt, ret_p, gla_p, conv_p = trunk(x_prompt, pos_p, ret0, gla0, conv0, *weights)
    pos_s = PAST_LEN + jnp.arange(x_sample.shape[1], dtype=jnp.int32)
    y_sample, ret_s, gla_s, conv_s = trunk(x_sample, pos_s, state_ret, state_gla, cache_conv, *weights)
    return (y_prompt, y_sample, ret_p, ret_s, gla_p, gla_s, conv_p, conv_s)
```

```python
import functools
import math

import numpy as np
import jax
import jax.numpy as jnp
from jax import lax
from jax.experimental import pallas as pl
from jax.experimental.pallas import tpu as pltpu

EPS = 1e-6
ROPE_BASE = 10000.0
GLA_TAU = 16.0
RET_HEADS = 4
GLA_HEADS = 4
GLA_GATE_RANK = 16
CONV_W = 3
PAST_LEN = 2048
GLA_CHUNK = 64
RET_CHUNK = 256
LANES = 128
VMEM_LIMIT_BYTES = 56 * 1024 * 1024

F32 = jnp.float32
BF16 = jnp.bfloat16


def _const_spec(shape):
    nd = len(shape)
    return pl.BlockSpec(shape, lambda *_: (0,) * nd, pipeline_mode=pl.Buffered(1))


def _rms(x, g):
    ms = jnp.mean(x * x, axis=-1, keepdims=True)
    return x * lax.rsqrt(ms + EPS) * g


def _silu(x):
    return x * (1.0 / (1.0 + jnp.exp(-x)))


def _dot(a, b):
    return jnp.dot(a, b, preferred_element_type=F32)


def _dot_nt(a, b):
    return lax.dot_general(a, b, (((1,), (1,)), ((), ())), preferred_element_type=F32)


def _dot_tn(a, b):
    return lax.dot_general(a, b, (((0,), (0,)), ((), ())), preferred_element_type=F32)


def _norm_proj_kernel(x_ref, g_ref, w_ref, o_ref, *, tn):
    h = _rms(x_ref[...], g_ref[...]).astype(BF16)
    n = w_ref.shape[1]
    for n0 in range(0, n, tn):
        n1 = min(n0 + tn, n)
        o_ref[:, n0:n1] = _dot(h, w_ref[:, n0:n1]).astype(o_ref.dtype)


def _norm_proj(x2, g, w, tm, tn=512):
    m, d = x2.shape
    n = w.shape[1]
    return pl.pallas_call(
        functools.partial(_norm_proj_kernel, tn=tn),
        grid=(m // tm,),
        in_specs=[
            pl.BlockSpec((tm, d), lambda i: (i, 0)),
            _const_spec((1, d)),
            _const_spec((d, n)),
        ],
        out_specs=pl.BlockSpec((tm, n), lambda i: (i, 0)),
        out_shape=jax.ShapeDtypeStruct((m, n), BF16),
        compiler_params=pltpu.CompilerParams(
            dimension_semantics=("parallel",), vmem_limit_bytes=VMEM_LIMIT_BYTES),
        name="norm_proj",
    )(x2, g, w)


def _ret_kernel(proj_ref, x_ref, cos_ref, sin_ref, s0_ref, dmat_ref, gn_ref, wout_ref,
                xo_ref, s_ref, y_scr, *, lc, heads, dk, dv, log_gamma):
    t = pl.program_id(1)

    @pl.when(t == 0)
    def _():
        s_ref[...] = s0_ref[...]

    tb = proj_ref.shape[0]
    half = dk // 2
    k_off = heads * dk
    v_off = 2 * heads * dk
    g_off = v_off + heads * dv
    scale = dk ** -0.5
    idx = lax.broadcasted_iota(jnp.int32, (lc, 1), 0).astype(F32)

    def rot(u, cos, sin):
        u1, u2 = u[:, :half], u[:, half:]
        return jnp.concatenate([u1 * cos - u2 * sin, u1 * sin + u2 * cos], axis=-1)

    for c in range(tb // lc):
        r0 = c * lc
        cos = cos_ref[r0:r0 + lc, :]
        sin = sin_ref[r0:r0 + lc, :]
        for h in range(heads):
            lg = log_gamma[h]
            q = rot(proj_ref[r0:r0 + lc, h * dk:(h + 1) * dk].astype(F32), cos, sin)
            k = rot(proj_ref[r0:r0 + lc, k_off + h * dk:k_off + (h + 1) * dk].astype(F32),
                    cos, sin) * scale
            v = proj_ref[r0:r0 + lc, v_off + h * dv:v_off + (h + 1) * dv]
            g = proj_ref[r0:r0 + lc, g_off + h * dv:g_off + (h + 1) * dv].astype(F32)
            s_old = s_ref[h]
            scores = _dot_nt(q.astype(BF16), k.astype(BF16)) * dmat_ref[h]
            o = _dot(scores.astype(BF16), v)
            q_dec = jnp.exp(lg * (idx + 1.0))
            o = o + _dot((q * q_dec).astype(BF16), s_old.astype(BF16))
            k_dec = jnp.exp(lg * (lc - 1.0 - idx))
            s_ref[h] = math.exp(lg * lc) * s_old + _dot_tn((k * k_dec).astype(BF16), v)
            mu = jnp.mean(o, axis=-1, keepdims=True)
            d = o - mu
            var = jnp.mean(d * d, axis=-1, keepdims=True)
            on = d * lax.rsqrt(var + EPS) * gn_ref[:, h * dv:(h + 1) * dv]
            y_scr[r0:r0 + lc, h * dv:(h + 1) * dv] = (_silu(g) * on).astype(BF16)

    xo_ref[...] = x_ref[...] + _dot(y_scr[...], wout_ref[...])


def _retention(proj, x, cos, sin, s0, gn, wout, tb, lc):
    b, t, d = x.shape
    heads, dk, dv = s0.shape[1:]
    n = proj.shape[-1]
    log_gamma = [math.log1p(-(2.0 ** (-5.0 - h))) for h in range(heads)]
    ii = np.arange(lc, dtype=np.float64)
    diff = ii[:, None] - ii[None, :]
    dmat = np.stack([np.where(diff >= 0, np.exp(lg * np.maximum(diff, 0.0)), 0.0)
                     for lg in log_gamma]).astype(np.float32)
    kern = functools.partial(_ret_kernel, lc=lc, heads=heads, dk=dk, dv=dv,
                             log_gamma=tuple(log_gamma))
    return pl.pallas_call(
        kern,
        grid=(b, t // tb),
        in_specs=[
            pl.BlockSpec((None, tb, n), lambda i, j: (i, j, 0)),
            pl.BlockSpec((None, tb, d), lambda i, j: (i, j, 0)),
            pl.BlockSpec((tb, dk // 2), lambda i, j: (j, 0)),
            pl.BlockSpec((tb, dk // 2), lambda i, j: (j, 0)),
            pl.BlockSpec((None, heads, dk, dv), lambda i, j: (i, 0, 0, 0)),
            _const_spec((heads, lc, lc)),
            _const_spec((1, heads * dv)),
            _const_spec((heads * dv, d)),
        ],
        out_specs=[
            pl.BlockSpec((None, tb, d), lambda i, j: (i, j, 0)),
            pl.BlockSpec((None, heads, dk, dv), lambda i, j: (i, 0, 0, 0)),
        ],
        out_shape=[
            jax.ShapeDtypeStruct((b, t, d), F32),
            jax.ShapeDtypeStruct(s0.shape, F32),
        ],
        scratch_shapes=[pltpu.VMEM((tb, heads * dv), BF16)],
        compiler_params=pltpu.CompilerParams(
            dimension_semantics=("parallel", "arbitrary"), vmem_limit_bytes=VMEM_LIMIT_BYTES),
        name="retention",
    )(proj, x, cos, sin, s0, jnp.asarray(dmat), gn, wout)


def _gla_tables(lc):
    nlev = int(math.log2(lc))
    t = np.arange(lc)[None, :]
    i = np.arange(lc)[:, None]
    blocks = [(t <= i), (t > i)]
    masks = []
    for lev in range(nlev):
        h = lc >> (lev + 1)
        m = (i // (2 * h)) * 2 * h + h - 1
        second = i > m
        blocks.append(np.where(second, (t > m) & (t <= i), (t > i) & (t <= m)))
        j = np.arange(lc)[None, :]
        same = (i // (2 * h)) == (j // (2 * h))
        masks.append(same & second & (j <= (j // (2 * h)) * 2 * h + h - 1))
    masks.append(i == np.arange(lc)[None, :])
    mall = np.concatenate(blocks, axis=0).astype(np.float32)
    return mall, np.stack(masks).astype(np.float32), nlev


def _gla_kernel(proj_ref, x_ref, s0_ref, wa2_ref, ba_ref, mall_ref, masks_ref, ng_ref, wout_ref,
                xo_ref, s_ref, st_scr, y_scr, *, lc, nlev, heads, dk, dv):
    t = pl.program_id(1)
    nt = pl.num_programs(1)

    @pl.when(t == 0)
    def _():
        for h in range(heads):
            st_scr[h] = s0_ref[h].T

    tb = proj_ref.shape[0]
    qk = heads * dk
    v_off = 2 * qk
    r_off = v_off + heads * dv
    a_off = r_off + heads * dv
    scale = dk ** -0.5
    row = lax.broadcasted_iota(jnp.int32, (lc, 1), 0)

    def chunk(c, carry):
        r0 = pl.multiple_of(c * lc, lc)
        rows = pl.ds(r0, lc)
        q = proj_ref[rows, 0:qk].astype(F32)
        k = proj_ref[rows, qk:2 * qk].astype(F32) * scale
        a = proj_ref[rows, a_off:a_off + LANES]
        z = _dot(a, wa2_ref[...]) + ba_ref[...]
        lg = (jnp.minimum(z, 0.0) - jnp.log1p(jnp.exp(-jnp.abs(z)))) * (1.0 / GLA_TAU)
        lg_hi = lg.astype(BF16)
        lg_lo = (lg - lg_hi.astype(F32)).astype(BF16)
        mall = mall_ref[...]
        f_all = jnp.exp(_dot(mall, lg_hi) + _dot(mall, lg_lo))

        qx = (q * f_all[0:lc]).astype(BF16)
        kx = (k * f_all[lc:2 * lc]).astype(BF16)
        f_last = f_all[lc - 1:lc]
        ql, kl = [], []
        for lev in range(nlev):
            hbit = int(math.log2(lc >> (lev + 1)))
            second = ((row >> hbit) & 1) == 1
            f_lev = f_all[(2 + lev) * lc:(3 + lev) * lc]
            ql.append(jnp.where(second, q * f_lev, 0.0).astype(BF16))
            kl.append(jnp.where(second, 0.0, k * f_lev).astype(BF16))
        ql.append(q.astype(BF16))
        kl.append(k.astype(BF16))

        for h in range(heads):
            ks = slice(h * dk, (h + 1) * dk)
            amat = jnp.zeros((lc, lc), F32)
            for lev in range(nlev + 1):
                amat = amat + masks_ref[lev] * _dot_nt(ql[lev][:, ks], kl[lev][:, ks])
            v = proj_ref[rows, v_off + h * dv:v_off + (h + 1) * dv]
            r = proj_ref[rows, r_off + h * dv:r_off + (h + 1) * dv].astype(F32)
            st_old = st_scr[h]
            o = _dot(amat.astype(BF16), v) + _dot_nt(qx[:, ks], st_old.astype(BF16))
            st_scr[h] = st_old * f_last[:, ks] + _dot_tn(v, kx[:, ks])
            ms = jnp.mean(o * o, axis=-1, keepdims=True)
            on = o * lax.rsqrt(ms + EPS) * ng_ref[:, h * dv:(h + 1) * dv]
            y_scr[rows, h * dv:(h + 1) * dv] = (_silu(r) * on).astype(BF16)
        return carry

    lax.fori_loop(0, tb // lc, chunk, 0)

    xo_ref[...] = x_ref[...] + _dot(y_scr[...], wout_ref[...])

    @pl.when(t == nt - 1)
    def _():
        for h in range(heads):
            s_ref[h] = st_scr[h].T


def _gla(proj, x, s0, wa2p, ba, ng, wout, tb, lc):
    b, t, d = x.shape
    heads, dk, dv = s0.shape[1:]
    n = proj.shape[-1]
    mall, masks, nlev = _gla_tables(lc)
    kern = functools.partial(_gla_kernel, lc=lc, nlev=nlev, heads=heads, dk=dk, dv=dv)
    return pl.pallas_call(
        kern,
        grid=(b, t // tb),
        in_specs=[
            pl.BlockSpec((None, tb, n), lambda i, j: (i, j, 0)),
            pl.BlockSpec((None, tb, d), lambda i, j: (i, j, 0)),
            pl.BlockSpec((None, heads, dk, dv), lambda i, j: (i, 0, 0, 0)),
            _const_spec(wa2p.shape),
            _const_spec((1, heads * dk)),
            _const_spec(mall.shape),
            _const_spec(masks.shape),
            _const_spec((1, heads * dv)),
            _const_spec((heads * dv, d)),
        ],
        out_specs=[
            pl.BlockSpec((None, tb, d), lambda i, j: (i, j, 0)),
            pl.BlockSpec((None, heads, dk, dv), lambda i, j: (i, 0, 0, 0)),
        ],
        out_shape=[
            jax.ShapeDtypeStruct((b, t, d), F32),
            jax.ShapeDtypeStruct(s0.shape, F32),
        ],
        scratch_shapes=[
            pltpu.VMEM((heads, dv, dk), F32),
            pltpu.VMEM((tb, heads * dv), BF16),
        ],
        compiler_params=pltpu.CompilerParams(
            dimension_semantics=("parallel", "arbitrary"), vmem_limit_bytes=VMEM_LIMIT_BYTES),
        name="gla",
    )(proj, x, s0, wa2p, ba, jnp.asarray(mall, BF16), jnp.asarray(masks), ng, wout)


def _ffn_kernel(x_ref, g_ref, wup_ref, cw_ref, cb_ref, wdn_ref, cs_ref, gf_ref,
                xo_ref, nc_ref, *, cwid, final_norm):
    t = pl.program_id(1)

    @pl.when(t == 0)
    def _():
        nc_ref[...] = cs_ref[...]

    x = x_ref[...]
    tm = x.shape[0]
    dff = wdn_ref.shape[0]
    h = _rms(x, g_ref[...]).astype(BF16)
    row = lax.broadcasted_iota(jnp.int32, (tm, 1), 0)

    def conv(c0):
        cols = slice(c0, c0 + cwid)
        u = _dot(h, wup_ref[:, cols])
        p0 = nc_ref[0:1, cols]
        p1 = nc_ref[1:2, cols]
        u1 = jnp.where(row == 0, p1, pltpu.roll(u, 1, 0))
        u2 = jnp.where(row == 0, p0, jnp.where(row == 1, p1, pltpu.roll(u, 2, 0)))
        nc_ref[:, cols] = u[tm - 2:tm, :]
        return (cb_ref[:, cols] + u2 * cw_ref[0:1, cols] + u1 * cw_ref[1:2, cols]
                + u * cw_ref[2:3, cols])

    acc = x
    for c in range(dff // cwid):
        gate = conv(c * cwid)
        val = conv(dff + c * cwid)
        act = (_silu(gate) * val).astype(BF16)
        acc = acc + _dot(act, wdn_ref[c * cwid:(c + 1) * cwid, :])
    if final_norm:
        acc = _rms(acc, gf_ref[...])
    xo_ref[...] = acc


def _conv_ffn(x, g, wup, cw, cb, wdn, cs, gf, tm, final_norm, cwid=256):
    b, t, d = x.shape
    dff = wdn.shape[0]
    kern = functools.partial(_ffn_kernel, cwid=cwid, final_norm=final_norm)
    return pl.pallas_call(
        kern,
        grid=(b, t // tm),
        in_specs=[
            pl.BlockSpec((None, tm, d), lambda i, j: (i, j, 0)),
            _const_spec((1, d)),
            _const_spec((d, 2 * dff)),
            _const_spec((CONV_W, 2 * dff)),
            _const_spec((1, 2 * dff)),
            _const_spec((dff, d)),
            pl.BlockSpec((None, CONV_W - 1, 2 * dff), lambda i, j: (i, 0, 0)),
            _const_spec((1, d)),
        ],
        out_specs=[
            pl.BlockSpec((None, tm, d), lambda i, j: (i, j, 0)),
            pl.BlockSpec((None, CONV_W - 1, 2 * dff), lambda i, j: (i, 0, 0)),
        ],
        out_shape=[
            jax.ShapeDtypeStruct((b, t, d), F32),
            jax.ShapeDtypeStruct((b, CONV_W - 1, 2 * dff), F32),
        ],
        compiler_params=pltpu.CompilerParams(
            dimension_semantics=("parallel", "arbitrary"), vmem_limit_bytes=VMEM_LIMIT_BYTES),
        name="conv_ffn",
    )(x, g, wup, cw, cb, wdn, cs, gf)


def _tile(t, pref):
    return pref if t % pref == 0 else t


def _trunk(x, pos0, ret_state, gla_state, conv_state, w):
    b, t, d = x.shape
    tm = _tile(b * t, 512)
    tb = _tile(t, 512)
    ret_lc = _tile(tb, RET_CHUNK)
    gla_lc = _tile(tb, GLA_CHUNK)

    half = ret_state.shape[-2] // 2
    inv = ROPE_BASE ** (-jnp.arange(half, dtype=F32) / half)
    ang = (pos0 + jnp.arange(t, dtype=jnp.int32)).astype(F32)[:, None] * inv[None, :]
    cos, sin = jnp.cos(ang), jnp.sin(ang)

    proj = _norm_proj(x.reshape(b * t, d), w["norm_mix"][0:1], w["ret_w_in"], tm)
    x, ret_s = _retention(proj.reshape(b, t, -1), x, cos, sin, ret_state, w["ret_gn_g"],
                          w["ret_w_out"], tb, ret_lc)
    x, conv0 = _conv_ffn(x, w["norm_ffn"][0:1], w["ffn_w_up"][0], w["ffn_conv_w"][0],
                         w["ffn_conv_b"][0:1], w["ffn_w_down"][0], conv_state[0],
                         w["norm_final"], tb, False)
    proj = _norm_proj(x.reshape(b * t, d), w["norm_mix"][1:2], w["gla_w_in"], tm)
    x, gla_s = _gla(proj.reshape(b, t, -1), x, gla_state, w["gla_w_a2"], w["gla_b_a"],
                    w["gla_norm_g"], w["gla_w_out"], tb, gla_lc)
    x, conv1 = _conv_ffn(x, w["norm_ffn"][1:2], w["ffn_w_up"][1], w["ffn_conv_w"][1],
                         w["ffn_conv_b"][1:2], w["ffn_w_down"][1], conv_state[1],
                         w["norm_final"], tb, True)
    return x, ret_s[None], gla_s[None], jnp.stack([conv0, conv1])


def kernel(x_prompt, x_sample, state_ret, state_gla, cache_conv, norm_mix, norm_ffn, norm_final,
           ret_w_in, ret_gn_g, ret_w_out, gla_w_in, gla_w_a2, gla_b_a, gla_norm_g, gla_w_out,
           ffn_w_up, ffn_conv_w, ffn_conv_b, ffn_w_down):
    assert norm_mix.shape[0] == 2 and ret_w_in.shape[0] == 1 and gla_w_in.shape[0] == 1
    d = x_prompt.shape[-1]
    gla_qk = gla_w_a2.shape[-1]
    gla_v = gla_w_out.shape[1]
    n_main = 2 * gla_qk + 2 * gla_v
    gla_w_in_p = jnp.concatenate(
        [gla_w_in[0, :, :n_main],
         jnp.pad(gla_w_in[0, :, n_main:], ((0, 0), (0, LANES - GLA_GATE_RANK)))], axis=1)
    w = dict(
        norm_mix=norm_mix, norm_ffn=norm_ffn, norm_final=norm_final.reshape(1, d),
        ret_w_in=ret_w_in[0].astype(BF16),
        ret_gn_g=ret_gn_g[0].reshape(1, -1),
        ret_w_out=ret_w_out[0].astype(BF16),
        gla_w_in=gla_w_in_p.astype(BF16),
        gla_w_a2=jnp.pad(gla_w_a2[0], ((0, LANES - GLA_GATE_RANK), (0, 0))).astype(BF16),
        gla_b_a=gla_b_a[0].reshape(1, -1),
        gla_norm_g=gla_norm_g[0].reshape(1, -1),
        gla_w_out=gla_w_out[0].astype(BF16),
        ffn_w_up=ffn_w_up.astype(BF16), ffn_conv_w=ffn_conv_w, ffn_conv_b=ffn_conv_b,
        ffn_w_down=ffn_w_down.astype(BF16),
    )
    bp = x_prompt.shape[0]
    ret0 = jnp.zeros((bp,) + state_ret.shape[2:], F32)
    gla0 = jnp.zeros((bp,) + state_gla.shape[2:], F32)
    conv0 = jnp.zeros((cache_conv.shape[0], bp) + cache_conv.shape[2:], F32)
    y_p, ret_p, gla_p, conv_p = _trunk(x_prompt, 0, ret0, gla0, conv0, w)
    y_s, ret_s, gla_s, conv_s = _trunk(x_sample, PAST_LEN, state_ret[0], state_gla[0],
                                       cache_conv, w)
    return (y_p, y_s, ret_p, ret_s, gla_p, gla_s, conv_p, conv_s)
```

```python
import functools
import math

import numpy as np
import jax
import jax.numpy as jnp
from jax import lax
from jax.experimental import pallas as pl
from jax.experimental.pallas import tpu as pltpu

EPS = 1e-6
ROPE_BASE = 10000.0
GLA_TAU = 16.0
RET_HEADS = 4
GLA_HEADS = 4
GLA_GATE_RANK = 16
CONV_W = 3
PAST_LEN = 2048
GLA_CHUNK = 64
RET_CHUNK = 256
LANES = 128
SUBLANES = 8
VMEM_LIMIT_BYTES = 56 * 1024 * 1024

F32 = jnp.float32
BF16 = jnp.bfloat16


def _const_spec(shape):
    nd = len(shape)
    return pl.BlockSpec(shape, lambda *_: (0,) * nd, pipeline_mode=pl.Buffered(1))


def _rms(x, g):
    ms = jnp.mean(x * x, axis=-1, keepdims=True)
    return x * lax.rsqrt(ms + EPS) * g


def _silu(x):
    return x * (1.0 / (1.0 + jnp.exp(-x)))


def _dot(a, b):
    return jnp.dot(a, b, preferred_element_type=F32)


def _dot_nt(a, b):
    return lax.dot_general(a, b, (((1,), (1,)), ((), ())), preferred_element_type=F32)


def _dot_tn(a, b):
    return lax.dot_general(a, b, (((0,), (0,)), ((), ())), preferred_element_type=F32)


def _norm_proj_kernel(x_ref, g_ref, w_ref, o_ref, *, tn):
    h = _rms(x_ref[...], g_ref[...]).astype(BF16)
    n = w_ref.shape[1]
    for n0 in range(0, n, tn):
        n1 = min(n0 + tn, n)
        o_ref[:, n0:n1] = _dot(h, w_ref[:, n0:n1]).astype(o_ref.dtype)


def _norm_proj(x2, g, w, tm, tn=512):
    m, d = x2.shape
    n = w.shape[1]
    return pl.pallas_call(
        functools.partial(_norm_proj_kernel, tn=tn),
        grid=(m // tm,),
        in_specs=[
            pl.BlockSpec((tm, d), lambda i: (i, 0)),
            _const_spec((1, d)),
            _const_spec((d, n)),
        ],
        out_specs=pl.BlockSpec((tm, n), lambda i: (i, 0)),
        out_shape=jax.ShapeDtypeStruct((m, n), BF16),
        compiler_params=pltpu.CompilerParams(
            dimension_semantics=("parallel",), vmem_limit_bytes=VMEM_LIMIT_BYTES),
        name="norm_proj",
    )(x2, g, w)


def _ret_kernel(proj_ref, x_ref, cos_ref, sin_ref, s0_ref, dmat_ref, gn_ref, wout_ref,
                xo_ref, s_ref, y_scr, *, lc, heads, dk, dv, log_gamma):
    t = pl.program_id(1)

    @pl.when(t == 0)
    def _():
        s_ref[...] = s0_ref[...]

    tb = proj_ref.shape[0]
    half = dk // 2
    k_off = heads * dk
    v_off = 2 * heads * dk
    g_off = v_off + heads * dv
    scale = dk ** -0.5
    idx = lax.broadcasted_iota(jnp.int32, (lc, 1), 0).astype(F32)

    def rot(u, cos, sin):
        u1, u2 = u[:, :half], u[:, half:]
        return jnp.concatenate([u1 * cos - u2 * sin, u1 * sin + u2 * cos], axis=-1)

    for c in range(tb // lc):
        r0 = c * lc
        cos = cos_ref[r0:r0 + lc, :]
        sin = sin_ref[r0:r0 + lc, :]
        for h in range(heads):
            lg = log_gamma[h]
            q = rot(proj_ref[r0:r0 + lc, h * dk:(h + 1) * dk].astype(F32), cos, sin)
            k = rot(proj_ref[r0:r0 + lc, k_off + h * dk:k_off + (h + 1) * dk].astype(F32),
                    cos, sin) * scale
            v = proj_ref[r0:r0 + lc, v_off + h * dv:v_off + (h + 1) * dv]
            g = proj_ref[r0:r0 + lc, g_off + h * dv:g_off + (h + 1) * dv].astype(F32)
            s_old = s_ref[h]
            scores = _dot_nt(q.astype(BF16), k.astype(BF16)) * dmat_ref[h]
            o = _dot(scores.astype(BF16), v)
            q_dec = jnp.exp(lg * (idx + 1.0))
            o = o + _dot((q * q_dec).astype(BF16), s_old.astype(BF16))
            k_dec = jnp.exp(lg * (lc - 1.0 - idx))
            s_ref[h] = math.exp(lg * lc) * s_old + _dot_tn((k * k_dec).astype(BF16), v)
            mu = jnp.mean(o, axis=-1, keepdims=True)
            d = o - mu
            var = jnp.mean(d * d, axis=-1, keepdims=True)
            on = d * lax.rsqrt(var + EPS) * gn_ref[:, h * dv:(h + 1) * dv]
            y_scr[r0:r0 + lc, h * dv:(h + 1) * dv] = (_silu(g) * on).astype(BF16)

    xo_ref[...] = x_ref[...] + _dot(y_scr[...], wout_ref[...])


def _retention(proj, x, cos, sin, s0, gn, wout, tb, lc):
    b, t, d = x.shape
    heads, dk, dv = s0.shape[1:]
    n = proj.shape[-1]
    log_gamma = [math.log1p(-(2.0 ** (-5.0 - h))) for h in range(heads)]
    ii = np.arange(lc, dtype=np.float64)
    diff = ii[:, None] - ii[None, :]
    dmat = np.stack([np.where(diff >= 0, np.exp(lg * np.maximum(diff, 0.0)), 0.0)
                     for lg in log_gamma]).astype(np.float32)
    kern = functools.partial(_ret_kernel, lc=lc, heads=heads, dk=dk, dv=dv,
                             log_gamma=tuple(log_gamma))
    return pl.pallas_call(
        kern,
        grid=(b, t // tb),
        in_specs=[
            pl.BlockSpec((None, tb, n), lambda i, j: (i, j, 0)),
            pl.BlockSpec((None, tb, d), lambda i, j: (i, j, 0)),
            pl.BlockSpec((tb, dk // 2), lambda i, j: (j, 0)),
            pl.BlockSpec((tb, dk // 2), lambda i, j: (j, 0)),
            pl.BlockSpec((None, heads, dk, dv), lambda i, j: (i, 0, 0, 0)),
            _const_spec((heads, lc, lc)),
            _const_spec((1, heads * dv)),
            _const_spec((heads * dv, d)),
        ],
        out_specs=[
            pl.BlockSpec((None, tb, d), lambda i, j: (i, j, 0)),
            pl.BlockSpec((None, heads, dk, dv), lambda i, j: (i, 0, 0, 0)),
        ],
        out_shape=[
            jax.ShapeDtypeStruct((b, t, d), F32),
            jax.ShapeDtypeStruct(s0.shape, F32),
        ],
        scratch_shapes=[pltpu.VMEM((tb, heads * dv), BF16)],
        compiler_params=pltpu.CompilerParams(
            dimension_semantics=("parallel", "arbitrary"), vmem_limit_bytes=VMEM_LIMIT_BYTES),
        name="retention",
    )(proj, x, cos, sin, s0, jnp.asarray(dmat), gn, wout)


def _gla_tables(lc):
    nlev = int(math.log2(lc))
    t = np.arange(lc)[None, :]
    i = np.arange(lc)[:, None]
    blocks = [(t <= i), (t > i)]
    masks = []
    for lev in range(nlev):
        h = lc >> (lev + 1)
        m = (i // (2 * h)) * 2 * h + h - 1
        second = i > m
        blocks.append(np.where(second, (t > m) & (t <= i), (t > i) & (t <= m)))
        j = np.arange(lc)[None, :]
        same = (i // (2 * h)) == (j // (2 * h))
        masks.append(same & second & (j <= (j // (2 * h)) * 2 * h + h - 1))
    masks.append(i == np.arange(lc)[None, :])
    mall = np.concatenate(blocks, axis=0).astype(np.float32)
    return mall, np.stack(masks).astype(np.float32), nlev


def _gla_kernel(proj_ref, x_ref, s0_ref, wa2_ref, ba_ref, mall_ref, masks_ref, ng_ref, wout_ref,
                xo_ref, s_ref, st_scr, y_scr, *, lc, nlev, heads, dk, dv):
    t = pl.program_id(1)
    nt = pl.num_programs(1)

    @pl.when(t == 0)
    def _():
        for h in range(heads):
            st_scr[h] = s0_ref[h].T

    tb = proj_ref.shape[0]
    qk = heads * dk
    v_off = 2 * qk
    r_off = v_off + heads * dv
    a_off = r_off + heads * dv
    scale = dk ** -0.5
    row = lax.broadcasted_iota(jnp.int32, (lc, 1), 0)

    def chunk(c, carry):
        r0 = pl.multiple_of(c * lc, lc)
        rows = pl.ds(r0, lc)
        q = proj_ref[rows, 0:qk].astype(F32)
        k = proj_ref[rows, qk:2 * qk].astype(F32) * scale
        a = proj_ref[rows, a_off:a_off + LANES]
        z = _dot(a, wa2_ref[...]) + ba_ref[...]
        lg = (jnp.minimum(z, 0.0) - jnp.log1p(jnp.exp(-jnp.abs(z)))) * (1.0 / GLA_TAU)
        lg_hi = lg.astype(BF16)
        lg_lo = (lg - lg_hi.astype(F32)).astype(BF16)
        mall = mall_ref[...]
        f_all = jnp.exp(_dot(mall, lg_hi) + _dot(mall, lg_lo))

        qx = (q * f_all[0:lc]).astype(BF16)
        kx = (k * f_all[lc:2 * lc]).astype(BF16)
        f_last = f_all[lc - 1:lc]
        ql, kl = [], []
        for lev in range(nlev):
            hbit = int(math.log2(lc >> (lev + 1)))
            second = ((row >> hbit) & 1) == 1
            f_lev = f_all[(2 + lev) * lc:(3 + lev) * lc]
            ql.append(jnp.where(second, q * f_lev, 0.0).astype(BF16))
            kl.append(jnp.where(second, 0.0, k * f_lev).astype(BF16))
        ql.append(q.astype(BF16))
        kl.append(k.astype(BF16))

        for h in range(heads):
            ks = slice(h * dk, (h + 1) * dk)
            amat = jnp.zeros((lc, lc), F32)
            for lev in range(nlev + 1):
                amat = amat + masks_ref[lev] * _dot_nt(ql[lev][:, ks], kl[lev][:, ks])
            v = proj_ref[rows, v_off + h * dv:v_off + (h + 1) * dv]
            r = proj_ref[rows, r_off + h * dv:r_off + (h + 1) * dv].astype(F32)
            st_old = st_scr[h]
            o = _dot(amat.astype(BF16), v) + _dot_nt(qx[:, ks], st_old.astype(BF16))
            st_scr[h] = st_old * f_last[:, ks] + _dot_tn(v, kx[:, ks])
            ms = jnp.mean(o * o, axis=-1, keepdims=True)
            on = o * lax.rsqrt(ms + EPS) * ng_ref[:, h * dv:(h + 1) * dv]
            y_scr[rows, h * dv:(h + 1) * dv] = (_silu(r) * on).astype(BF16)
        return carry

    lax.fori_loop(0, tb // lc, chunk, 0, unroll=min(4, tb // lc))

    xo_ref[...] = x_ref[...] + _dot(y_scr[...], wout_ref[...])

    @pl.when(t == nt - 1)
    def _():
        for h in range(heads):
            s_ref[h] = st_scr[h].T


def _gla(proj, x, s0, wa2p, ba, ng, wout, tb, lc):
    b, t, d = x.shape
    heads, dk, dv = s0.shape[1:]
    n = proj.shape[-1]
    mall, masks, nlev = _gla_tables(lc)
    kern = functools.partial(_gla_kernel, lc=lc, nlev=nlev, heads=heads, dk=dk, dv=dv)
    return pl.pallas_call(
        kern,
        grid=(b, t // tb),
        in_specs=[
            pl.BlockSpec((None, tb, n), lambda i, j: (i, j, 0)),
            pl.BlockSpec((None, tb, d), lambda i, j: (i, j, 0)),
            pl.BlockSpec((None, heads, dk, dv), lambda i, j: (i, 0, 0, 0)),
            _const_spec(wa2p.shape),
            _const_spec((1, heads * dk)),
            _const_spec(mall.shape),
            _const_spec(masks.shape),
            _const_spec((1, heads * dv)),
            _const_spec((heads * dv, d)),
        ],
        out_specs=[
            pl.BlockSpec((None, tb, d), lambda i, j: (i, j, 0)),
            pl.BlockSpec((None, heads, dk, dv), lambda i, j: (i, 0, 0, 0)),
        ],
        out_shape=[
            jax.ShapeDtypeStruct((b, t, d), F32),
            jax.ShapeDtypeStruct(s0.shape, F32),
        ],
        scratch_shapes=[
            pltpu.VMEM((heads, dv, dk), F32),
            pltpu.VMEM((tb, heads * dv), BF16),
        ],
        compiler_params=pltpu.CompilerParams(
            dimension_semantics=("parallel", "arbitrary"), vmem_limit_bytes=VMEM_LIMIT_BYTES),
        name="gla",
    )(proj, x, s0, wa2p, ba, jnp.asarray(mall, BF16), jnp.asarray(masks), ng, wout)


def _ffn_kernel(x_ref, g_ref, wup_ref, cw_ref, cb_ref, wdn_ref, cs_ref, gf_ref,
                xo_ref, nc_ref, ubuf, abuf, *, cwid, down_group, final_norm):
    t = pl.program_id(1)

    @pl.when(t == 0)
    def _():
        nc_ref[...] = cs_ref[...]

    x = x_ref[...]
    tm = x.shape[0]
    dff = wdn_ref.shape[0]
    h = _rms(x, g_ref[...]).astype(BF16)
    hdr = SUBLANES

    def conv(c0, slot):
        cols = slice(c0, c0 + cwid)
        u = _dot(h, wup_ref[:, cols])
        ub = ubuf.at[slot]
        ub[hdr - 2:hdr, :] = nc_ref[:, cols]
        ub[hdr:hdr + tm, :] = u
        nc_ref[:, cols] = u[tm - 2:tm, :]
        return (cb_ref[:, cols] + ub[hdr - 2:hdr - 2 + tm, :] * cw_ref[0:1, cols]
                + ub[hdr - 1:hdr - 1 + tm, :] * cw_ref[1:2, cols] + u * cw_ref[2:3, cols])

    nchunk = dff // cwid
    out = x
    k0 = 0
    for c in range(nchunk):
        gate = conv(c * cwid, 2 * (c % 2))
        val = conv(dff + c * cwid, 2 * (c % 2) + 1)
        abuf[:, c * cwid:(c + 1) * cwid] = (_silu(gate) * val).astype(BF16)
        if (c + 1) % down_group == 0 or c == nchunk - 1:
            k1 = (c + 1) * cwid
            out = out + _dot(abuf[:, k0:k1], wdn_ref[k0:k1, :])
            k0 = k1
    if final_norm:
        out = _rms(out, gf_ref[...])
    xo_ref[...] = out


def _conv_ffn(x, g, wup, cw, cb, wdn, cs, gf, tm, final_norm, cwid=256, down_group=11):
    b, t, d = x.shape
    dff = wdn.shape[0]
    kern = functools.partial(_ffn_kernel, cwid=cwid, down_group=down_group,
                             final_norm=final_norm)
    return pl.pallas_call(
        kern,
        grid=(b, t // tm),
        in_specs=[
            pl.BlockSpec((None, tm, d), lambda i, j: (i, j, 0)),
            _const_spec((1, d)),
            _const_spec((d, 2 * dff)),
            _const_spec((CONV_W, 2 * dff)),
            _const_spec((1, 2 * dff)),
            _const_spec((dff, d)),
            pl.BlockSpec((None, CONV_W - 1, 2 * dff), lambda i, j: (i, 0, 0)),
            _const_spec((1, d)),
        ],
        out_specs=[
            pl.BlockSpec((None, tm, d), lambda i, j: (i, j, 0)),
            pl.BlockSpec((None, CONV_W - 1, 2 * dff), lambda i, j: (i, 0, 0)),
        ],
        out_shape=[
            jax.ShapeDtypeStruct((b, t, d), F32),
            jax.ShapeDtypeStruct((b, CONV_W - 1, 2 * dff), F32),
        ],
        scratch_shapes=[
            pltpu.VMEM((4, tm + SUBLANES, cwid), F32),
            pltpu.VMEM((tm, dff), BF16),
        ],
        compiler_params=pltpu.CompilerParams(
            dimension_semantics=("parallel", "arbitrary"), vmem_limit_bytes=VMEM_LIMIT_BYTES),
        name="conv_ffn",
    )(x, g, wup, cw, cb, wdn, cs, gf)


def _tile(t, pref):
    return pref if t % pref == 0 else t


def _trunk(x, pos0, ret_state, gla_state, conv_state, w):
    b, t, d = x.shape
    tm = _tile(b * t, 512)
    tb = _tile(t, 512)
    ret_lc = _tile(tb, RET_CHUNK)
    gla_lc = _tile(tb, GLA_CHUNK)

    half = ret_state.shape[-2] // 2
    inv = ROPE_BASE ** (-jnp.arange(half, dtype=F32) / half)
    ang = (pos0 + jnp.arange(t, dtype=jnp.int32)).astype(F32)[:, None] * inv[None, :]
    cos, sin = jnp.cos(ang), jnp.sin(ang)

    proj = _norm_proj(x.reshape(b * t, d), w["norm_mix"][0:1], w["ret_w_in"], tm)
    x, ret_s = _retention(proj.reshape(b, t, -1), x, cos, sin, ret_state, w["ret_gn_g"],
                          w["ret_w_out"], tb, ret_lc)
    x, conv0 = _conv_ffn(x, w["norm_ffn"][0:1], w["ffn_w_up"][0], w["ffn_conv_w"][0],
                         w["ffn_conv_b"][0:1], w["ffn_w_down"][0], conv_state[0],
                         w["norm_final"], tb, False)
    proj = _norm_proj(x.reshape(b * t, d), w["norm_mix"][1:2], w["gla_w_in"], tm)
    x, gla_s = _gla(proj.reshape(b, t, -1), x, gla_state, w["gla_w_a2"], w["gla_b_a"],
                    w["gla_norm_g"], w["gla_w_out"], tb, gla_lc)
    x, conv1 = _conv_ffn(x, w["norm_ffn"][1:2], w["ffn_w_up"][1], w["ffn_conv_w"][1],
                         w["ffn_conv_b"][1:2], w["ffn_w_down"][1], conv_state[1],
                         w["norm_final"], tb, True)
    return x, ret_s[None], gla_s[None], jnp.stack([conv0, conv1])


def kernel(x_prompt, x_sample, state_ret, state_gla, cache_conv, norm_mix, norm_ffn, norm_final,
           ret_w_in, ret_gn_g, ret_w_out, gla_w_in, gla_w_a2, gla_b_a, gla_norm_g, gla_w_out,
           ffn_w_up, ffn_conv_w, ffn_conv_b, ffn_w_down):
    assert norm_mix.shape[0] == 2 and ret_w_in.shape[0] == 1 and gla_w_in.shape[0] == 1
    d = x_prompt.shape[-1]
    gla_qk = gla_w_a2.shape[-1]
    gla_v = gla_w_out.shape[1]
    n_main = 2 * gla_qk + 2 * gla_v
    gla_w_in_p = jnp.concatenate(
        [gla_w_in[0, :, :n_main],
         jnp.pad(gla_w_in[0, :, n_main:], ((0, 0), (0, LANES - GLA_GATE_RANK)))], axis=1)
    w = dict(
        norm_mix=norm_mix, norm_ffn=norm_ffn, norm_final=norm_final.reshape(1, d),
        ret_w_in=ret_w_in[0].astype(BF16),
        ret_gn_g=ret_gn_g[0].reshape(1, -1),
        ret_w_out=ret_w_out[0].astype(BF16),
        gla_w_in=gla_w_in_p.astype(BF16),
        gla_w_a2=jnp.pad(gla_w_a2[0], ((0, LANES - GLA_GATE_RANK), (0, 0))).astype(BF16),
        gla_b_a=gla_b_a[0].reshape(1, -1),
        gla_norm_g=gla_norm_g[0].reshape(1, -1),
        gla_w_out=gla_w_out[0].astype(BF16),
        ffn_w_up=ffn_w_up.astype(BF16), ffn_conv_w=ffn_conv_w, ffn_conv_b=ffn_conv_b,
        ffn_w_down=ffn_w_down.astype(BF16),
    )
    bp = x_prompt.shape[0]
    ret0 = jnp.zeros((bp,) + state_ret.shape[2:], F32)
    gla0 = jnp.zeros((bp,) + state_gla.shape[2:], F32)
    conv0 = jnp.zeros((cache_conv.shape[0], bp) + cache_conv.shape[2:], F32)
    y_p, ret_p, gla_p, conv_p = _trunk(x_prompt, 0, ret0, gla0, conv0, w)
    y_s, ret_s, gla_s, conv_s = _trunk(x_sample, PAST_LEN, state_ret[0], state_gla[0],
                                       cache_conv, w)
    return (y_p, y_s, ret_p, ret_s, gla_p, gla_s, conv_p, conv_s)
```

```python
import functools
import math

import numpy as np
import jax
import jax.numpy as jnp
from jax import lax
from jax.experimental import pallas as pl
from jax.experimental.pallas import tpu as pltpu

EPS = 1e-6
ROPE_BASE = 10000.0
GLA_TAU = 16.0
RET_HEADS = 4
GLA_HEADS = 4
GLA_GATE_RANK = 16
CONV_W = 3
PAST_LEN = 2048
GLA_CHUNK = 64
RET_CHUNK = 256
LANES = 128
SUBLANES = 8
VMEM_LIMIT_BYTES = 56 * 1024 * 1024

F32 = jnp.float32
BF16 = jnp.bfloat16


def _const_spec(shape):
    nd = len(shape)
    return pl.BlockSpec(shape, lambda *_: (0,) * nd, pipeline_mode=pl.Buffered(1))


def _rms(x, g):
    ms = jnp.mean(x * x, axis=-1, keepdims=True)
    return x * lax.rsqrt(ms + EPS) * g


def _silu(x):
    return x * (1.0 / (1.0 + jnp.exp(-x)))


def _dot(a, b):
    return jnp.dot(a, b, preferred_element_type=F32)


def _dot_nt(a, b):
    return lax.dot_general(a, b, (((1,), (1,)), ((), ())), preferred_element_type=F32)


def _dot_tn(a, b):
    return lax.dot_general(a, b, (((0,), (0,)), ((), ())), preferred_element_type=F32)


def _norm_proj_kernel(x_ref, g_ref, w_ref, o_ref, *, tn):
    h = _rms(x_ref[...], g_ref[...]).astype(BF16)
    n = w_ref.shape[1]
    for n0 in range(0, n, tn):
        n1 = min(n0 + tn, n)
        o_ref[:, n0:n1] = _dot(h, w_ref[:, n0:n1]).astype(o_ref.dtype)


def _norm_proj(x2, g, w, tm, tn=512):
    m, d = x2.shape
    n = w.shape[1]
    return pl.pallas_call(
        functools.partial(_norm_proj_kernel, tn=tn),
        grid=(m // tm,),
        in_specs=[
            pl.BlockSpec((tm, d), lambda i: (i, 0)),
            _const_spec((1, d)),
            _const_spec((d, n)),
        ],
        out_specs=pl.BlockSpec((tm, n), lambda i: (i, 0)),
        out_shape=jax.ShapeDtypeStruct((m, n), BF16),
        compiler_params=pltpu.CompilerParams(
            dimension_semantics=("parallel",), vmem_limit_bytes=VMEM_LIMIT_BYTES),
        name="norm_proj",
    )(x2, g, w)


def _ret_kernel(proj_ref, x_ref, cos_ref, sin_ref, s0_ref, dmat_ref, gn_ref, wout_ref,
                xo_ref, s_ref, y_scr, *, lc, heads, dk, dv, log_gamma):
    t = pl.program_id(1)

    @pl.when(t == 0)
    def _():
        s_ref[...] = s0_ref[...]

    tb = proj_ref.shape[0]
    half = dk // 2
    k_off = heads * dk
    v_off = 2 * heads * dk
    g_off = v_off + heads * dv
    scale = dk ** -0.5
    idx = lax.broadcasted_iota(jnp.int32, (lc, 1), 0).astype(F32)

    def rot(u, cos, sin):
        u1, u2 = u[:, :half], u[:, half:]
        return jnp.concatenate([u1 * cos - u2 * sin, u1 * sin + u2 * cos], axis=-1)

    for c in range(tb // lc):
        r0 = c * lc
        cos = cos_ref[r0:r0 + lc, :]
        sin = sin_ref[r0:r0 + lc, :]
        for h in range(heads):
            lg = log_gamma[h]
            q = rot(proj_ref[r0:r0 + lc, h * dk:(h + 1) * dk].astype(F32), cos, sin)
            k = rot(proj_ref[r0:r0 + lc, k_off + h * dk:k_off + (h + 1) * dk].astype(F32),
                    cos, sin) * scale
            v = proj_ref[r0:r0 + lc, v_off + h * dv:v_off + (h + 1) * dv]
            g = proj_ref[r0:r0 + lc, g_off + h * dv:g_off + (h + 1) * dv].astype(F32)
            s_old = s_ref[h]
            scores = _dot_nt(q.astype(BF16), k.astype(BF16)) * dmat_ref[h]
            o = _dot(scores.astype(BF16), v)
            q_dec = jnp.exp(lg * (idx + 1.0))
            o = o + _dot((q * q_dec).astype(BF16), s_old.astype(BF16))
            k_dec = jnp.exp(lg * (lc - 1.0 - idx))
            s_ref[h] = math.exp(lg * lc) * s_old + _dot_tn((k * k_dec).astype(BF16), v)
            mu = jnp.mean(o, axis=-1, keepdims=True)
            d = o - mu
            var = jnp.mean(d * d, axis=-1, keepdims=True)
            on = d * lax.rsqrt(var + EPS) * gn_ref[:, h * dv:(h + 1) * dv]
            y_scr[r0:r0 + lc, h * dv:(h + 1) * dv] = (_silu(g) * on).astype(BF16)

    xo_ref[...] = x_ref[...] + _dot(y_scr[...], wout_ref[...])


def _retention(proj, x, cos, sin, s0, gn, wout, tb, lc):
    b, t, d = x.shape
    heads, dk, dv = s0.shape[1:]
    n = proj.shape[-1]
    log_gamma = [math.log1p(-(2.0 ** (-5.0 - h))) for h in range(heads)]
    ii = np.arange(lc, dtype=np.float64)
    diff = ii[:, None] - ii[None, :]
    dmat = np.stack([np.where(diff >= 0, np.exp(lg * np.maximum(diff, 0.0)), 0.0)
                     for lg in log_gamma]).astype(np.float32)
    kern = functools.partial(_ret_kernel, lc=lc, heads=heads, dk=dk, dv=dv,
                             log_gamma=tuple(log_gamma))
    return pl.pallas_call(
        kern,
        grid=(b, t // tb),
        in_specs=[
            pl.BlockSpec((None, tb, n), lambda i, j: (i, j, 0)),
            pl.BlockSpec((None, tb, d), lambda i, j: (i, j, 0)),
            pl.BlockSpec((tb, dk // 2), lambda i, j: (j, 0)),
            pl.BlockSpec((tb, dk // 2), lambda i, j: (j, 0)),
            pl.BlockSpec((None, heads, dk, dv), lambda i, j: (i, 0, 0, 0)),
            _const_spec((heads, lc, lc)),
            _const_spec((1, heads * dv)),
            _const_spec((heads * dv, d)),
        ],
        out_specs=[
            pl.BlockSpec((None, tb, d), lambda i, j: (i, j, 0)),
            pl.BlockSpec((None, heads, dk, dv), lambda i, j: (i, 0, 0, 0)),
        ],
        out_shape=[
            jax.ShapeDtypeStruct((b, t, d), F32),
            jax.ShapeDtypeStruct(s0.shape, F32),
        ],
        scratch_shapes=[pltpu.VMEM((tb, heads * dv), BF16)],
        compiler_params=pltpu.CompilerParams(
            dimension_semantics=("parallel", "arbitrary"), vmem_limit_bytes=VMEM_LIMIT_BYTES),
        name="retention",
    )(proj, x, cos, sin, s0, jnp.asarray(dmat), gn, wout)


def _gla_tables(lc, heads):
    nlev = int(math.log2(lc))
    i = np.arange(lc)[:, None]
    j = np.arange(lc)[None, :]
    masks = []
    for lev in range(nlev):
        h = lc >> (lev + 1)
        same = (i // (2 * h)) == (j // (2 * h))
        masks.append(same & ((i // h) % 2 == 1) & ((j // h) % 2 == 0))
    masks.append(i == j)
    masks = np.tile(np.stack(masks).astype(np.float32), (1, 1, heads))
    tri = (j <= i).astype(np.float32)
    return tri, masks, nlev


def _gla_kernel(proj_ref, x_ref, s0_ref, wa2_ref, ba_ref, tri_ref, masks_ref, ng_ref, wout_ref,
                xo_ref, s_ref, y_scr, *, lc, nlev, heads, dk, dv):
    t = pl.program_id(1)

    @pl.when(t == 0)
    def _():
        s_ref[...] = s0_ref[...]

    tb = proj_ref.shape[0]
    qk = heads * dk
    v_off = 2 * qk
    r_off = v_off + heads * dv
    a_off = r_off + heads * dv
    scale = dk ** -0.5
    row = lax.broadcasted_iota(jnp.int32, (tb, 1), 0)
    zero_k = jnp.zeros((lc, dk), BF16)

    def blockdiag(kk):
        return jnp.concatenate(
            [jnp.concatenate([kk[:, h * dk:(h + 1) * dk] if g == h else zero_k
                              for g in range(heads)], axis=1) for h in range(heads)], axis=0)

    def block_rows(x, period, r):
        return jnp.concatenate(
            [jnp.broadcast_to(x[m:m + 1, :], (period, x.shape[1]))
             for m in range(r, x.shape[0], period)], axis=0)

    def level_exponent(h, b, lg):
        if h == 1:
            return jnp.where((row & 1) == 1, lg, 0.0)
        if h == 2:
            r4 = row & 3
            lg_next = pltpu.roll(lg, tb - 1, 0)
            lg_prev = pltpu.roll(lg, 1, 0)
            return jnp.where(r4 == 0, lg_next,
                             jnp.where(r4 == 1, 0.0, jnp.where(r4 == 2, lg, lg_prev + lg)))
        d = b - block_rows(b, 2 * h, h - 1)
        return jnp.where(((row // h) & 1) == 1, d, -d)

    nc = tb // lc
    chunks = [slice(c * lc, (c + 1) * lc) for c in range(nc)]
    q = proj_ref[:, 0:qk].astype(F32)
    k = proj_ref[:, qk:2 * qk].astype(F32) * scale
    z = _dot(proj_ref[:, a_off:a_off + LANES], wa2_ref[...]) + ba_ref[...]
    lg = (jnp.minimum(z, 0.0) - jnp.log(1.0 + jnp.exp(-jnp.abs(z)))) * (1.0 / GLA_TAU)
    lg_hi = lg.astype(BF16)
    lg_lo = (lg - lg_hi.astype(F32)).astype(BF16)
    tri = tri_ref[...]
    b = jnp.concatenate([_dot(tri, lg_hi[rs]) + _dot(tri, lg_lo[rs]) for rs in chunks], axis=0)
    b_last = block_rows(b, lc, lc - 1)
    qx = (q * jnp.exp(b)).astype(BF16)
    kx = (k * jnp.exp(b_last - b)).astype(BF16)
    f_last = jnp.exp(b_last)

    qb, kb = q.astype(BF16), k.astype(BF16)
    p = [masks_ref[nlev] * _dot_nt(qb[rs], blockdiag(kb[rs])) for rs in chunks]
    for lev in range(nlev):
        h = lc >> (lev + 1)
        second = ((row // h) & 1) == 1
        m = (jnp.where(second, q, k) * jnp.exp(level_exponent(h, b, lg))).astype(BF16)
        for c, rs in enumerate(chunks):
            p[c] = p[c] + masks_ref[lev] * _dot_nt(m[rs], blockdiag(m[rs]))

    v = [[proj_ref[rs, v_off + h * dv:v_off + (h + 1) * dv] for h in range(heads)]
         for rs in chunks]
    ds = [[_dot_tn(kx[rs, h * dk:(h + 1) * dk], v[c][h]) for h in range(heads)]
          for c, rs in enumerate(chunks)]

    s = [s_ref[h] for h in range(heads)]
    for c, rs in enumerate(chunks):
        p_bf = p[c].astype(BF16)
        for h in range(heads):
            ks = slice(h * dk, (h + 1) * dk)
            g0 = (h * lc) // LANES * LANES
            off = h * lc - g0
            lhs = jnp.concatenate([qx[rs, ks], p_bf[:, g0:g0 + LANES]], axis=1)
            rhs = [s[h].astype(BF16)]
            if off:
                rhs.append(jnp.zeros((off, dv), BF16))
            rhs.append(v[c][h])
            if LANES - off - lc:
                rhs.append(jnp.zeros((LANES - off - lc, dv), BF16))
            o = _dot(lhs, jnp.concatenate(rhs, axis=0))
            fl = f_last[c * lc:c * lc + 1, ks]
            decay = jnp.broadcast_to(fl, (dk, dk)).T
            s[h] = s[h] * jnp.concatenate([decay] * (dv // dk), axis=1) + ds[c][h]
            r = proj_ref[rs, r_off + h * dv:r_off + (h + 1) * dv].astype(F32)
            ms = jnp.mean(o * o, axis=-1, keepdims=True)
            on = o * lax.rsqrt(ms + EPS) * ng_ref[:, h * dv:(h + 1) * dv]
            y_scr[rs, h * dv:(h + 1) * dv] = (_silu(r) * on).astype(BF16)
    for h in range(heads):
        s_ref[h] = s[h]

    xo_ref[...] = x_ref[...] + _dot(y_scr[...], wout_ref[...])


def _gla(proj, x, s0, wa2p, ba, ng, wout, tb, lc):
    b, t, d = x.shape
    heads, dk, dv = s0.shape[1:]
    n = proj.shape[-1]
    tri, masks, nlev = _gla_tables(lc, heads)
    kern = functools.partial(_gla_kernel, lc=lc, nlev=nlev, heads=heads, dk=dk, dv=dv)
    return pl.pallas_call(
        kern,
        grid=(b, t // tb),
        in_specs=[
            pl.BlockSpec((None, tb, n), lambda i, j: (i, j, 0)),
            pl.BlockSpec((None, tb, d), lambda i, j: (i, j, 0)),
            pl.BlockSpec((None, heads, dk, dv), lambda i, j: (i, 0, 0, 0)),
            _const_spec(wa2p.shape),
            _const_spec((1, heads * dk)),
            _const_spec(tri.shape),
            _const_spec(masks.shape),
            _const_spec((1, heads * dv)),
            _const_spec((heads * dv, d)),
        ],
        out_specs=[
            pl.BlockSpec((None, tb, d), lambda i, j: (i, j, 0)),
            pl.BlockSpec((None, heads, dk, dv), lambda i, j: (i, 0, 0, 0)),
        ],
        out_shape=[
            jax.ShapeDtypeStruct((b, t, d), F32),
            jax.ShapeDtypeStruct(s0.shape, F32),
        ],
        scratch_shapes=[pltpu.VMEM((tb, heads * dv), BF16)],
        compiler_params=pltpu.CompilerParams(
            dimension_semantics=("parallel", "arbitrary"), vmem_limit_bytes=VMEM_LIMIT_BYTES),
        name="gla",
    )(proj, x, s0, wa2p, ba, jnp.asarray(tri, BF16), jnp.asarray(masks), ng, wout)


def _ffn_kernel(x_ref, g_ref, wup_ref, cw_ref, cb_ref, wdn_ref, cs_ref, gf_ref,
                xo_ref, nc_ref, ubuf, abuf, *, cwid, down_group, final_norm):
    t = pl.program_id(1)

    @pl.when(t == 0)
    def _():
        nc_ref[...] = cs_ref[...]

    x = x_ref[...]
    tm = x.shape[0]
    dff = wdn_ref.shape[0]
    h = _rms(x, g_ref[...]).astype(BF16)
    hdr = SUBLANES

    def conv(c0, slot):
        cols = slice(c0, c0 + cwid)
        u = _dot(h, wup_ref[:, cols])
        ub = ubuf.at[slot]
        ub[hdr - 2:hdr, :] = nc_ref[:, cols]
        ub[hdr:hdr + tm, :] = u
        nc_ref[:, cols] = u[tm - 2:tm, :]
        return (cb_ref[:, cols] + ub[hdr - 2:hdr - 2 + tm, :] * cw_ref[0:1, cols]
                + ub[hdr - 1:hdr - 1 + tm, :] * cw_ref[1:2, cols] + u * cw_ref[2:3, cols])

    nchunk = dff // cwid
    out = x
    k0 = 0
    for c in range(nchunk):
        gate = conv(c * cwid, 2 * (c % 2))
        val = conv(dff + c * cwid, 2 * (c % 2) + 1)
        abuf[:, c * cwid:(c + 1) * cwid] = (_silu(gate) * val).astype(BF16)
        if (c + 1) % down_group == 0 or c == nchunk - 1:
            k1 = (c + 1) * cwid
            out = out + _dot(abuf[:, k0:k1], wdn_ref[k0:k1, :])
            k0 = k1
    if final_norm:
        out = _rms(out, gf_ref[...])
    xo_ref[...] = out


def _conv_ffn(x, g, wup, cw, cb, wdn, cs, gf, tm, final_norm, cwid=256, down_group=11):
    b, t, d = x.shape
    dff = wdn.shape[0]
    kern = functools.partial(_ffn_kernel, cwid=cwid, down_group=down_group,
                             final_norm=final_norm)
    return pl.pallas_call(
        kern,
        grid=(b, t // tm),
        in_specs=[
            pl.BlockSpec((None, tm, d), lambda i, j: (i, j, 0)),
            _const_spec((1, d)),
            _const_spec((d, 2 * dff)),
            _const_spec((CONV_W, 2 * dff)),
            _const_spec((1, 2 * dff)),
            _const_spec((dff, d)),
            pl.BlockSpec((None, CONV_W - 1, 2 * dff), lambda i, j: (i, 0, 0)),
            _const_spec((1, d)),
        ],
        out_specs=[
            pl.BlockSpec((None, tm, d), lambda i, j: (i, j, 0)),
            pl.BlockSpec((None, CONV_W - 1, 2 * dff), lambda i, j: (i, 0, 0)),
        ],
        out_shape=[
            jax.ShapeDtypeStruct((b, t, d), F32),
            jax.ShapeDtypeStruct((b, CONV_W - 1, 2 * dff), F32),
        ],
        scratch_shapes=[
            pltpu.VMEM((4, tm + SUBLANES, cwid), F32),
            pltpu.VMEM((tm, dff), BF16),
        ],
        compiler_params=pltpu.CompilerParams(
            dimension_semantics=("parallel", "arbitrary"), vmem_limit_bytes=VMEM_LIMIT_BYTES),
        name="conv_ffn",
    )(x, g, wup, cw, cb, wdn, cs, gf)


def _tile(t, pref):
    return pref if t % pref == 0 else t


def _trunk(x, pos0, ret_state, gla_state, conv_state, w):
    b, t, d = x.shape
    tm = _tile(b * t, 512)
    tb = _tile(t, 512)
    ret_lc = _tile(tb, RET_CHUNK)
    gla_lc = _tile(tb, GLA_CHUNK)

    half = ret_state.shape[-2] // 2
    inv = ROPE_BASE ** (-jnp.arange(half, dtype=F32) / half)
    ang = (pos0 + jnp.arange(t, dtype=jnp.int32)).astype(F32)[:, None] * inv[None, :]
    cos, sin = jnp.cos(ang), jnp.sin(ang)

    proj = _norm_proj(x.reshape(b * t, d), w["norm_mix"][0:1], w["ret_w_in"], tm)
    x, ret_s = _retention(proj.reshape(b, t, -1), x, cos, sin, ret_state, w["ret_gn_g"],
                          w["ret_w_out"], tb, ret_lc)
    x, conv0 = _conv_ffn(x, w["norm_ffn"][0:1], w["ffn_w_up"][0], w["ffn_conv_w"][0],
                         w["ffn_conv_b"][0:1], w["ffn_w_down"][0], conv_state[0],
                         w["norm_final"], tb, False)
    proj = _norm_proj(x.reshape(b * t, d), w["norm_mix"][1:2], w["gla_w_in"], tm)
    x, gla_s = _gla(proj.reshape(b, t, -1), x, gla_state, w["gla_w_a2"], w["gla_b_a"],
                    w["gla_norm_g"], w["gla_w_out"], tb, gla_lc)
    x, conv1 = _conv_ffn(x, w["norm_ffn"][1:2], w["ffn_w_up"][1], w["ffn_conv_w"][1],
                         w["ffn_conv_b"][1:2], w["ffn_w_down"][1], conv_state[1],
                         w["norm_final"], tb, True)
    return x, ret_s[None], gla_s[None], jnp.stack([conv0, conv1])


def kernel(x_prompt, x_sample, state_ret, state_gla, cache_conv, norm_mix, norm_ffn, norm_final,
           ret_w_in, ret_gn_g, ret_w_out, gla_w_in, gla_w_a2, gla_b_a, gla_norm_g, gla_w_out,
           ffn_w_up, ffn_conv_w, ffn_conv_b, ffn_w_down):
    assert norm_mix.shape[0] == 2 and ret_w_in.shape[0] == 1 and gla_w_in.shape[0] == 1
    d = x_prompt.shape[-1]
    gla_qk = gla_w_a2.shape[-1]
    gla_v = gla_w_out.shape[1]
    n_main = 2 * gla_qk + 2 * gla_v
    gla_w_in_p = jnp.concatenate(
        [gla_w_in[0, :, :n_main],
         jnp.pad(gla_w_in[0, :, n_main:], ((0, 0), (0, LANES - GLA_GATE_RANK)))], axis=1)
    w = dict(
        norm_mix=norm_mix, norm_ffn=norm_ffn, norm_final=norm_final.reshape(1, d),
        ret_w_in=ret_w_in[0].astype(BF16),
        ret_gn_g=ret_gn_g[0].reshape(1, -1),
        ret_w_out=ret_w_out[0].astype(BF16),
        gla_w_in=gla_w_in_p.astype(BF16),
        gla_w_a2=jnp.pad(gla_w_a2[0], ((0, LANES - GLA_GATE_RANK), (0, 0))).astype(BF16),
        gla_b_a=gla_b_a[0].reshape(1, -1),
        gla_norm_g=gla_norm_g[0].reshape(1, -1),
        gla_w_out=gla_w_out[0].astype(BF16),
        ffn_w_up=ffn_w_up.astype(BF16), ffn_conv_w=ffn_conv_w, ffn_conv_b=ffn_conv_b,
        ffn_w_down=ffn_w_down.astype(BF16),
    )
    bp = x_prompt.shape[0]
    ret0 = jnp.zeros((bp,) + state_ret.shape[2:], F32)
    gla0 = jnp.zeros((bp,) + state_gla.shape[2:], F32)
    conv0 = jnp.zeros((cache_conv.shape[0], bp) + cache_conv.shape[2:], F32)
    y_p, ret_p, gla_p, conv_p = _trunk(x_prompt, 0, ret0, gla0, conv0, w)
    y_s, ret_s, gla_s, conv_s = _trunk(x_sample, PAST_LEN, state_ret[0], state_gla[0],
                                       cache_conv, w)
    return (y_p, y_s, ret_p, ret_s, gla_p, gla_s, conv_p, conv_s)
```

```python
import functools
import math

import numpy as np
import jax
import jax.numpy as jnp
from jax import lax
from jax.experimental import pallas as pl
from jax.experimental.pallas import tpu as pltpu

EPS = 1e-6
ROPE_BASE = 10000.0
GLA_TAU = 16.0
RET_HEADS = 4
GLA_HEADS = 4
GLA_GATE_RANK = 16
CONV_W = 3
PAST_LEN = 2048
GLA_CHUNK = 64
RET_CHUNK = 256
FFN_TILE = 512
GLA_TILE = 512
LANES = 128
SUBLANES = 8
VMEM_LIMIT_BYTES = 56 * 1024 * 1024

F32 = jnp.float32
BF16 = jnp.bfloat16


def _const_spec(shape):
    nd = len(shape)
    return pl.BlockSpec(shape, lambda *_: (0,) * nd, pipeline_mode=pl.Buffered(1))


def _rms(x, g):
    ms = jnp.mean(x * x, axis=-1, keepdims=True)
    return x * lax.rsqrt(ms + EPS) * g


def _silu(x):
    return x * (1.0 / (1.0 + jnp.exp(-x)))


def _dot(a, b):
    return jnp.dot(a, b, preferred_element_type=F32)


def _dot_nt(a, b):
    return lax.dot_general(a, b, (((1,), (1,)), ((), ())), preferred_element_type=F32)


def _dot_tn(a, b):
    return lax.dot_general(a, b, (((0,), (0,)), ((), ())), preferred_element_type=F32)


def _norm_proj_kernel(x_ref, g_ref, w_ref, o_ref, *, tn):
    h = _rms(x_ref[...], g_ref[...]).astype(BF16)
    n = w_ref.shape[1]
    for n0 in range(0, n, tn):
        n1 = min(n0 + tn, n)
        o_ref[:, n0:n1] = _dot(h, w_ref[:, n0:n1]).astype(o_ref.dtype)


def _norm_proj(x2, g, w, tm, tn=512):
    m, d = x2.shape
    n = w.shape[1]
    return pl.pallas_call(
        functools.partial(_norm_proj_kernel, tn=tn),
        grid=(m // tm,),
        in_specs=[
            pl.BlockSpec((tm, d), lambda i: (i, 0)),
            _const_spec((1, d)),
            _const_spec((d, n)),
        ],
        out_specs=pl.BlockSpec((tm, n), lambda i: (i, 0)),
        out_shape=jax.ShapeDtypeStruct((m, n), BF16),
        compiler_params=pltpu.CompilerParams(
            dimension_semantics=("parallel",), vmem_limit_bytes=VMEM_LIMIT_BYTES),
        name="norm_proj",
    )(x2, g, w)


def _ret_kernel(proj_ref, x_ref, cos_ref, sin_ref, s0_ref, dmat_ref, gn_ref, wout_ref,
                xo_ref, s_ref, y_scr, *, lc, heads, dk, dv, log_gamma):
    t = pl.program_id(1)

    @pl.when(t == 0)
    def _():
        s_ref[...] = s0_ref[...]

    tb = proj_ref.shape[0]
    half = dk // 2
    k_off = heads * dk
    v_off = 2 * heads * dk
    g_off = v_off + heads * dv
    scale = dk ** -0.5
    idx = lax.broadcasted_iota(jnp.int32, (lc, 1), 0).astype(F32)

    def rot(u, cos, sin):
        u1, u2 = u[:, :half], u[:, half:]
        return jnp.concatenate([u1 * cos - u2 * sin, u1 * sin + u2 * cos], axis=-1)

    for c in range(tb // lc):
        r0 = c * lc
        cos = cos_ref[r0:r0 + lc, :]
        sin = sin_ref[r0:r0 + lc, :]
        for h in range(heads):
            lg = log_gamma[h]
            q = rot(proj_ref[r0:r0 + lc, h * dk:(h + 1) * dk].astype(F32), cos, sin)
            k = rot(proj_ref[r0:r0 + lc, k_off + h * dk:k_off + (h + 1) * dk].astype(F32),
                    cos, sin) * scale
            v = proj_ref[r0:r0 + lc, v_off + h * dv:v_off + (h + 1) * dv]
            g = proj_ref[r0:r0 + lc, g_off + h * dv:g_off + (h + 1) * dv].astype(F32)
            s_old = s_ref[h]
            scores = _dot_nt(q.astype(BF16), k.astype(BF16)) * dmat_ref[h]
            o = _dot(scores.astype(BF16), v)
            q_dec = jnp.exp(lg * (idx + 1.0))
            o = o + _dot((q * q_dec).astype(BF16), s_old.astype(BF16))
            k_dec = jnp.exp(lg * (lc - 1.0 - idx))
            s_ref[h] = math.exp(lg * lc) * s_old + _dot_tn((k * k_dec).astype(BF16), v)
            mu = jnp.mean(o, axis=-1, keepdims=True)
            d = o - mu
            var = jnp.mean(d * d, axis=-1, keepdims=True)
            on = d * lax.rsqrt(var + EPS) * gn_ref[:, h * dv:(h + 1) * dv]
            y_scr[r0:r0 + lc, h * dv:(h + 1) * dv] = (_silu(g) * on).astype(BF16)

    xo_ref[...] = x_ref[...] + _dot(y_scr[...], wout_ref[...])


def _retention(proj, x, cos, sin, s0, gn, wout, tb, lc):
    b, t, d = x.shape
    heads, dk, dv = s0.shape[1:]
    n = proj.shape[-1]
    log_gamma = [math.log1p(-(2.0 ** (-5.0 - h))) for h in range(heads)]
    ii = np.arange(lc, dtype=np.float64)
    diff = ii[:, None] - ii[None, :]
    dmat = np.stack([np.where(diff >= 0, np.exp(lg * np.maximum(diff, 0.0)), 0.0)
                     for lg in log_gamma]).astype(np.float32)
    kern = functools.partial(_ret_kernel, lc=lc, heads=heads, dk=dk, dv=dv,
                             log_gamma=tuple(log_gamma))
    return pl.pallas_call(
        kern,
        grid=(b, t // tb),
        in_specs=[
            pl.BlockSpec((None, tb, n), lambda i, j: (i, j, 0)),
            pl.BlockSpec((None, tb, d), lambda i, j: (i, j, 0)),
            pl.BlockSpec((tb, dk // 2), lambda i, j: (j, 0)),
            pl.BlockSpec((tb, dk // 2), lambda i, j: (j, 0)),
            pl.BlockSpec((None, heads, dk, dv), lambda i, j: (i, 0, 0, 0)),
            _const_spec((heads, lc, lc)),
            _const_spec((1, heads * dv)),
            _const_spec((heads * dv, d)),
        ],
        out_specs=[
            pl.BlockSpec((None, tb, d), lambda i, j: (i, j, 0)),
            pl.BlockSpec((None, heads, dk, dv), lambda i, j: (i, 0, 0, 0)),
        ],
        out_shape=[
            jax.ShapeDtypeStruct((b, t, d), F32),
            jax.ShapeDtypeStruct(s0.shape, F32),
        ],
        scratch_shapes=[pltpu.VMEM((tb, heads * dv), BF16)],
        compiler_params=pltpu.CompilerParams(
            dimension_semantics=("parallel", "arbitrary"), vmem_limit_bytes=VMEM_LIMIT_BYTES),
        name="retention",
    )(proj, x, cos, sin, s0, jnp.asarray(dmat), gn, wout)


def _gla_tables(lc, heads):
    nlev = int(math.log2(lc))
    i = np.arange(lc)[:, None]
    j = np.arange(lc)[None, :]
    masks = []
    for lev in range(nlev):
        h = lc >> (lev + 1)
        same = (i // (2 * h)) == (j // (2 * h))
        masks.append(same & ((i // h) % 2 == 1) & ((j // h) % 2 == 0))
    masks.append(i == j)
    masks = np.tile(np.stack(masks).astype(np.float32), (1, 1, heads))
    tri = (j <= i).astype(np.float32)
    return tri, masks, nlev


def _gla_kernel(proj_ref, x_ref, s0_ref, wa2_ref, ba_ref, tri_ref, masks_ref, ng_ref, wout_ref,
                xo_ref, s_ref, y_scr, *, lc, nlev, heads, dk, dv):
    t = pl.program_id(1)

    @pl.when(t == 0)
    def _():
        s_ref[...] = s0_ref[...]

    tb = proj_ref.shape[0]
    qk = heads * dk
    v_off = 2 * qk
    r_off = v_off + heads * dv
    a_off = r_off + heads * dv
    scale = dk ** -0.5
    row = lax.broadcasted_iota(jnp.int32, (tb, 1), 0)
    zero_k = jnp.zeros((lc, dk), BF16)

    def blockdiag(kk):
        return jnp.concatenate(
            [jnp.concatenate([kk[:, h * dk:(h + 1) * dk] if g == h else zero_k
                              for g in range(heads)], axis=1) for h in range(heads)], axis=0)

    def block_rows(x, period, r):
        return jnp.concatenate(
            [jnp.broadcast_to(x[m:m + 1, :], (period, x.shape[1]))
             for m in range(r, x.shape[0], period)], axis=0)

    def level_exponent(h, b, lg):
        if h == 1:
            return jnp.where((row & 1) == 1, lg, 0.0)
        if h == 2:
            r4 = row & 3
            lg_next = pltpu.roll(lg, tb - 1, 0)
            lg_prev = pltpu.roll(lg, 1, 0)
            return jnp.where(r4 == 0, lg_next,
                             jnp.where(r4 == 1, 0.0, jnp.where(r4 == 2, lg, lg_prev + lg)))
        return -jnp.abs(b - block_rows(b, 2 * h, h - 1))

    nc = tb // lc
    chunks = [slice(c * lc, (c + 1) * lc) for c in range(nc)]
    q = proj_ref[:, 0:qk].astype(F32)
    k = proj_ref[:, qk:2 * qk].astype(F32) * scale
    z = _dot(proj_ref[:, a_off:a_off + LANES], wa2_ref[...]) + ba_ref[...]
    lg = (jnp.minimum(z, 0.0) - jnp.log(1.0 + jnp.exp(-jnp.abs(z)))) * (1.0 / GLA_TAU)
    lg_hi = lg.astype(BF16)
    lg_lo = (lg - lg_hi.astype(F32)).astype(BF16)
    tri = tri_ref[...]
    b = jnp.concatenate([_dot(tri, lg_hi[rs]) + _dot(tri, lg_lo[rs]) for rs in chunks], axis=0)
    b_last = block_rows(b, lc, lc - 1)
    qx = (q * jnp.exp(b)).astype(BF16)
    kx = (k * jnp.exp(b_last - b)).astype(BF16)
    f_last = jnp.exp(b_last)

    qb, kb = q.astype(BF16), k.astype(BF16)
    p = [masks_ref[nlev] * _dot_nt(qb[rs], blockdiag(kb[rs])) for rs in chunks]
    for lev in range(nlev):
        h = lc >> (lev + 1)
        second = ((row // h) & 1) == 1
        m = (jnp.where(second, q, k) * jnp.exp(level_exponent(h, b, lg))).astype(BF16)
        for c, rs in enumerate(chunks):
            p[c] = p[c] + masks_ref[lev] * _dot_nt(m[rs], blockdiag(m[rs]))

    v = [[proj_ref[rs, v_off + h * dv:v_off + (h + 1) * dv] for h in range(heads)]
         for rs in chunks]
    ds = [[_dot_tn(kx[rs, h * dk:(h + 1) * dk], v[c][h]) for h in range(heads)]
          for c, rs in enumerate(chunks)]

    s = [s_ref[h] for h in range(heads)]
    for c, rs in enumerate(chunks):
        p_bf = p[c].astype(BF16)
        for h in range(heads):
            ks = slice(h * dk, (h + 1) * dk)
            g0 = (h * lc) // LANES * LANES
            off = h * lc - g0
            lhs = jnp.concatenate([qx[rs, ks], p_bf[:, g0:g0 + LANES]], axis=1)
            rhs = [s[h].astype(BF16)]
            if off:
                rhs.append(jnp.zeros((off, dv), BF16))
            rhs.append(v[c][h])
            if LANES - off - lc:
                rhs.append(jnp.zeros((LANES - off - lc, dv), BF16))
            o = _dot(lhs, jnp.concatenate(rhs, axis=0))
            fl = f_last[c * lc:c * lc + 1, ks]
            decay = jnp.broadcast_to(fl, (dk, dk)).T
            s[h] = s[h] * jnp.concatenate([decay] * (dv // dk), axis=1) + ds[c][h]
            r = proj_ref[rs, r_off + h * dv:r_off + (h + 1) * dv].astype(F32)
            ms = jnp.mean(o * o, axis=-1, keepdims=True)
            on = o * lax.rsqrt(ms + EPS) * ng_ref[:, h * dv:(h + 1) * dv]
            y_scr[rs, h * dv:(h + 1) * dv] = (_silu(r) * on).astype(BF16)
    for h in range(heads):
        s_ref[h] = s[h]

    xo_ref[...] = x_ref[...] + _dot(y_scr[...], wout_ref[...])


def _gla(proj, x, s0, wa2p, ba, ng, wout, tb, lc):
    b, t, d = x.shape
    heads, dk, dv = s0.shape[1:]
    n = proj.shape[-1]
    tri, masks, nlev = _gla_tables(lc, heads)
    kern = functools.partial(_gla_kernel, lc=lc, nlev=nlev, heads=heads, dk=dk, dv=dv)
    return pl.pallas_call(
        kern,
        grid=(b, t // tb),
        in_specs=[
            pl.BlockSpec((None, tb, n), lambda i, j: (i, j, 0)),
            pl.BlockSpec((None, tb, d), lambda i, j: (i, j, 0)),
            pl.BlockSpec((None, heads, dk, dv), lambda i, j: (i, 0, 0, 0)),
            _const_spec(wa2p.shape),
            _const_spec((1, heads * dk)),
            _const_spec(tri.shape),
            _const_spec(masks.shape),
            _const_spec((1, heads * dv)),
            _const_spec((heads * dv, d)),
        ],
        out_specs=[
            pl.BlockSpec((None, tb, d), lambda i, j: (i, j, 0)),
            pl.BlockSpec((None, heads, dk, dv), lambda i, j: (i, 0, 0, 0)),
        ],
        out_shape=[
            jax.ShapeDtypeStruct((b, t, d), F32),
            jax.ShapeDtypeStruct(s0.shape, F32),
        ],
        scratch_shapes=[pltpu.VMEM((tb, heads * dv), BF16)],
        compiler_params=pltpu.CompilerParams(
            dimension_semantics=("parallel", "arbitrary"), vmem_limit_bytes=VMEM_LIMIT_BYTES),
        name="gla",
    )(proj, x, s0, wa2p, ba, jnp.asarray(tri, BF16), jnp.asarray(masks), ng, wout)


def _ffn_kernel(x_ref, g_ref, wup_ref, cw_ref, cb_ref, wdn_ref, cs_ref, gf_ref,
                xo_ref, nc_ref, hbuf, ubuf, abuf, *, cwid, final_norm):
    t = pl.program_id(1)

    @pl.when(t == 0)
    def _():
        nc_ref[...] = cs_ref[...]

    tm = x_ref.shape[0]
    dff = wdn_ref.shape[0]
    hbuf[...] = _rms(x_ref[...], g_ref[...]).astype(BF16)
    hdr = SUBLANES

    def conv(c0, slot):
        cols = slice(c0, c0 + cwid)
        u = _dot(hbuf[...], wup_ref[:, cols])
        ub = ubuf.at[slot]
        ub[hdr - 2:hdr, :] = nc_ref[:, cols]
        ub[hdr:hdr + tm, :] = u
        nc_ref[:, cols] = u[tm - 2:tm, :]
        return (cb_ref[:, cols] + ub[hdr - 2:hdr - 2 + tm, :] * cw_ref[0:1, cols]
                + ub[hdr - 1:hdr - 1 + tm, :] * cw_ref[1:2, cols] + u * cw_ref[2:3, cols])

    for c in range(dff // cwid):
        gate = conv(c * cwid, 2 * (c % 2))
        val = conv(dff + c * cwid, 2 * (c % 2) + 1)
        abuf[:, c * cwid:(c + 1) * cwid] = (_silu(gate) * val).astype(BF16)
    out = x_ref[...] + _dot(abuf[...], wdn_ref[...])
    if final_norm:
        out = _rms(out, gf_ref[...])
    xo_ref[...] = out


def _conv_ffn(x, g, wup, cw, cb, wdn, cs, gf, tm, final_norm, cwid=256):
    b, t, d = x.shape
    dff = wdn.shape[0]
    kern = functools.partial(_ffn_kernel, cwid=cwid, final_norm=final_norm)
    return pl.pallas_call(
        kern,
        grid=(b, t // tm),
        in_specs=[
            pl.BlockSpec((None, tm, d), lambda i, j: (i, j, 0)),
            _const_spec((1, d)),
            _const_spec((d, 2 * dff)),
            _const_spec((CONV_W, 2 * dff)),
            _const_spec((1, 2 * dff)),
            _const_spec((dff, d)),
            pl.BlockSpec((None, CONV_W - 1, 2 * dff), lambda i, j: (i, 0, 0)),
            _const_spec((1, d)),
        ],
        out_specs=[
            pl.BlockSpec((None, tm, d), lambda i, j: (i, j, 0)),
            pl.BlockSpec((None, CONV_W - 1, 2 * dff), lambda i, j: (i, 0, 0)),
        ],
        out_shape=[
            jax.ShapeDtypeStruct((b, t, d), F32),
            jax.ShapeDtypeStruct((b, CONV_W - 1, 2 * dff), F32),
        ],
        scratch_shapes=[
            pltpu.VMEM((tm, d), BF16),
            pltpu.VMEM((4, tm + SUBLANES, cwid), F32),
            pltpu.VMEM((tm, dff), BF16),
        ],
        compiler_params=pltpu.CompilerParams(
            dimension_semantics=("parallel", "arbitrary"), vmem_limit_bytes=VMEM_LIMIT_BYTES),
        name="conv_ffn",
    )(x, g, wup, cw, cb, wdn, cs, gf)


def _tile(t, pref):
    return pref if t % pref == 0 else t


def _trunk(x, pos0, ret_state, gla_state, conv_state, w):
    b, t, d = x.shape
    tm = _tile(b * t, 512)
    tb = _tile(t, 512)
    ret_lc = _tile(tb, RET_CHUNK)
    gla_tb = _tile(t, GLA_TILE)
    gla_lc = _tile(gla_tb, GLA_CHUNK)

    half = ret_state.shape[-2] // 2
    inv = ROPE_BASE ** (-jnp.arange(half, dtype=F32) / half)
    blk = min(t, 64)
    ang_a = (pos0 + blk * jnp.arange(t // blk, dtype=jnp.int32)).astype(F32)[:, None] * inv[None, :]
    ang_r = jnp.arange(blk, dtype=F32)[:, None] * inv[None, :]
    ca, sa = jnp.cos(ang_a)[:, None, :], jnp.sin(ang_a)[:, None, :]
    cr, sr = jnp.cos(ang_r)[None, :, :], jnp.sin(ang_r)[None, :, :]
    cos = (ca * cr - sa * sr).reshape(t, half)
    sin = (sa * cr + ca * sr).reshape(t, half)

    proj = _norm_proj(x.reshape(b * t, d), w["norm_mix"][0:1], w["ret_w_in"], tm)
    x, ret_s = _retention(proj.reshape(b, t, -1), x, cos, sin, ret_state, w["ret_gn_g"],
                          w["ret_w_out"], tb, ret_lc)
    x, conv0 = _conv_ffn(x, w["norm_ffn"][0:1], w["ffn_w_up"][0], w["ffn_conv_w"][0],
                         w["ffn_conv_b"][0:1], w["ffn_w_down"][0], conv_state[0],
                         w["norm_final"], _tile(t, FFN_TILE), False)
    proj = _norm_proj(x.reshape(b * t, d), w["norm_mix"][1:2], w["gla_w_in"], tm)
    x, gla_s = _gla(proj.reshape(b, t, -1), x, gla_state, w["gla_w_a2"], w["gla_b_a"],
                    w["gla_norm_g"], w["gla_w_out"], gla_tb, gla_lc)
    x, conv1 = _conv_ffn(x, w["norm_ffn"][1:2], w["ffn_w_up"][1], w["ffn_conv_w"][1],
                         w["ffn_conv_b"][1:2], w["ffn_w_down"][1], conv_state[1],
                         w["norm_final"], _tile(t, FFN_TILE), True)
    return x, ret_s[None], gla_s[None], jnp.stack([conv0, conv1])


def kernel(x_prompt, x_sample, state_ret, state_gla, cache_conv, norm_mix, norm_ffn, norm_final,
           ret_w_in, ret_gn_g, ret_w_out, gla_w_in, gla_w_a2, gla_b_a, gla_norm_g, gla_w_out,
           ffn_w_up, ffn_conv_w, ffn_conv_b, ffn_w_down):
    assert norm_mix.shape[0] == 2 and ret_w_in.shape[0] == 1 and gla_w_in.shape[0] == 1
    d = x_prompt.shape[-1]
    gla_qk = gla_w_a2.shape[-1]
    gla_v = gla_w_out.shape[1]
    n_main = 2 * gla_qk + 2 * gla_v
    gla_w_in_p = jnp.concatenate(
        [gla_w_in[0, :, :n_main],
         jnp.pad(gla_w_in[0, :, n_main:], ((0, 0), (0, LANES - GLA_GATE_RANK)))], axis=1)
    w = dict(
        norm_mix=norm_mix, norm_ffn=norm_ffn, norm_final=norm_final.reshape(1, d),
        ret_w_in=ret_w_in[0].astype(BF16),
        ret_gn_g=ret_gn_g[0].reshape(1, -1),
        ret_w_out=ret_w_out[0].astype(BF16),
        gla_w_in=gla_w_in_p.astype(BF16),
        gla_w_a2=jnp.pad(gla_w_a2[0], ((0, LANES - GLA_GATE_RANK), (0, 0))).astype(BF16),
        gla_b_a=gla_b_a[0].reshape(1, -1),
        gla_norm_g=gla_norm_g[0].reshape(1, -1),
        gla_w_out=gla_w_out[0].astype(BF16),
        ffn_w_up=[ffn_w_up[i].astype(BF16) for i in range(ffn_w_up.shape[0])],
        ffn_conv_w=ffn_conv_w, ffn_conv_b=ffn_conv_b,
        ffn_w_down=[ffn_w_down[i].astype(BF16) for i in range(ffn_w_down.shape[0])],
    )
    bp = x_prompt.shape[0]
    ret0 = jnp.zeros((bp,) + state_ret.shape[2:], F32)
    gla0 = jnp.zeros((bp,) + state_gla.shape[2:], F32)
    conv0 = jnp.zeros((cache_conv.shape[0], bp) + cache_conv.shape[2:], F32)
    y_p, ret_p, gla_p, conv_p = _trunk(x_prompt, 0, ret0, gla0, conv0, w)
    y_s, ret_s, gla_s, conv_s = _trunk(x_sample, PAST_LEN, state_ret[0], state_gla[0],
                                       cache_conv, w)
    return (y_p, y_s, ret_p, ret_s, gla_p, gla_s, conv_p, conv_s)
```

```python
import functools
import math

import numpy as np
import jax
import jax.numpy as jnp
from jax import lax
from jax.experimental import pallas as pl
from jax.experimental.pallas import tpu as pltpu

EPS = 1e-6
ROPE_BASE = 10000.0
GLA_TAU = 16.0
RET_HEADS = 4
GLA_HEADS = 4
GLA_GATE_RANK = 16
CONV_W = 3
PAST_LEN = 2048
GLA_CHUNK = 64
RET_CHUNK = 256
FFN_TILE = 512
GLA_TILE = 512
LANES = 128
SUBLANES = 8
VMEM_LIMIT_BYTES = 56 * 1024 * 1024

F32 = jnp.float32
BF16 = jnp.bfloat16


def _const_spec(shape):
    nd = len(shape)
    return pl.BlockSpec(shape, lambda *_: (0,) * nd, pipeline_mode=pl.Buffered(1))


def _layer_spec(shape, layer):
    nd = len(shape)
    return pl.BlockSpec((None,) + tuple(shape), lambda *_: (layer,) + (0,) * nd,
                        pipeline_mode=pl.Buffered(1))


def _rms(x, g):
    ms = jnp.mean(x * x, axis=-1, keepdims=True)
    return x * lax.rsqrt(ms + EPS) * g


def _silu(x):
    return x * (1.0 / (1.0 + jnp.exp(-x)))


def _dot(a, b):
    return jnp.dot(a, b, preferred_element_type=F32)


def _dot_nt(a, b):
    return lax.dot_general(a, b, (((1,), (1,)), ((), ())), preferred_element_type=F32)


def _dot_tn(a, b):
    return lax.dot_general(a, b, (((0,), (0,)), ((), ())), preferred_element_type=F32)


def _norm_proj_kernel(x_ref, g_ref, w_ref, o_ref, *, tn):
    h = _rms(x_ref[...], g_ref[...]).astype(BF16)
    n = w_ref.shape[1]
    for n0 in range(0, n, tn):
        n1 = min(n0 + tn, n)
        o_ref[:, n0:n1] = _dot(h, w_ref[:, n0:n1]).astype(o_ref.dtype)


def _norm_proj(x2, g, w, tm, tn=512):
    m, d = x2.shape
    n = w.shape[1]
    return pl.pallas_call(
        functools.partial(_norm_proj_kernel, tn=tn),
        grid=(m // tm,),
        in_specs=[
            pl.BlockSpec((tm, d), lambda i: (i, 0)),
            _const_spec((1, d)),
            _const_spec((d, n)),
        ],
        out_specs=pl.BlockSpec((tm, n), lambda i: (i, 0)),
        out_shape=jax.ShapeDtypeStruct((m, n), BF16),
        compiler_params=pltpu.CompilerParams(
            dimension_semantics=("parallel",), vmem_limit_bytes=VMEM_LIMIT_BYTES),
        name="norm_proj",
    )(x2, g, w)


def _ret_kernel(proj_ref, x_ref, cos_ref, sin_ref, s0_ref, dmat_ref, gn_ref, wout_ref,
                xo_ref, s_ref, y_scr, *, lc, heads, dk, dv, log_gamma):
    t = pl.program_id(1)

    @pl.when(t == 0)
    def _():
        s_ref[...] = s0_ref[...]

    tb = proj_ref.shape[0]
    half = dk // 2
    k_off = heads * dk
    v_off = 2 * heads * dk
    g_off = v_off + heads * dv
    scale = dk ** -0.5
    idx = lax.broadcasted_iota(jnp.int32, (lc, 1), 0).astype(F32)

    def rot(u, cos, sin):
        u1, u2 = u[:, :half], u[:, half:]
        return jnp.concatenate([u1 * cos - u2 * sin, u1 * sin + u2 * cos], axis=-1)

    for c in range(tb // lc):
        r0 = c * lc
        cos = cos_ref[r0:r0 + lc, :]
        sin = sin_ref[r0:r0 + lc, :]
        for h in range(heads):
            lg = log_gamma[h]
            q = rot(proj_ref[r0:r0 + lc, h * dk:(h + 1) * dk].astype(F32), cos, sin)
            k = rot(proj_ref[r0:r0 + lc, k_off + h * dk:k_off + (h + 1) * dk].astype(F32),
                    cos, sin) * scale
            v = proj_ref[r0:r0 + lc, v_off + h * dv:v_off + (h + 1) * dv]
            g = proj_ref[r0:r0 + lc, g_off + h * dv:g_off + (h + 1) * dv].astype(F32)
            s_old = s_ref[h]
            scores = _dot_nt(q.astype(BF16), k.astype(BF16)) * dmat_ref[h]
            o = _dot(scores.astype(BF16), v)
            q_dec = jnp.exp(lg * (idx + 1.0))
            o = o + _dot((q * q_dec).astype(BF16), s_old.astype(BF16))
            k_dec = jnp.exp(lg * (lc - 1.0 - idx))
            s_ref[h] = math.exp(lg * lc) * s_old + _dot_tn((k * k_dec).astype(BF16), v)
            mu = jnp.mean(o, axis=-1, keepdims=True)
            d = o - mu
            var = jnp.mean(d * d, axis=-1, keepdims=True)
            on = d * lax.rsqrt(var + EPS) * gn_ref[:, h * dv:(h + 1) * dv]
            y_scr[r0:r0 + lc, h * dv:(h + 1) * dv] = (_silu(g) * on).astype(BF16)

    xo_ref[...] = x_ref[...] + _dot(y_scr[...], wout_ref[...])


def _retention(proj, x, cos, sin, s0, gn, wout, tb, lc):
    b, t, d = x.shape
    heads, dk, dv = s0.shape[1:]
    n = proj.shape[-1]
    log_gamma = [math.log1p(-(2.0 ** (-5.0 - h))) for h in range(heads)]
    ii = np.arange(lc, dtype=np.float64)
    diff = ii[:, None] - ii[None, :]
    dmat = np.stack([np.where(diff >= 0, np.exp(lg * np.maximum(diff, 0.0)), 0.0)
                     for lg in log_gamma]).astype(np.float32)
    kern = functools.partial(_ret_kernel, lc=lc, heads=heads, dk=dk, dv=dv,
                             log_gamma=tuple(log_gamma))
    return pl.pallas_call(
        kern,
        grid=(b, t // tb),
        in_specs=[
            pl.BlockSpec((None, tb, n), lambda i, j: (i, j, 0)),
            pl.BlockSpec((None, tb, d), lambda i, j: (i, j, 0)),
            pl.BlockSpec((tb, dk // 2), lambda i, j: (j, 0)),
            pl.BlockSpec((tb, dk // 2), lambda i, j: (j, 0)),
            pl.BlockSpec((None, heads, dk, dv), lambda i, j: (i, 0, 0, 0)),
            _const_spec((heads, lc, lc)),
            _const_spec((1, heads * dv)),
            _const_spec((heads * dv, d)),
        ],
        out_specs=[
            pl.BlockSpec((None, tb, d), lambda i, j: (i, j, 0)),
            pl.BlockSpec((None, heads, dk, dv), lambda i, j: (i, 0, 0, 0)),
        ],
        out_shape=[
            jax.ShapeDtypeStruct((b, t, d), F32),
            jax.ShapeDtypeStruct(s0.shape, F32),
        ],
        scratch_shapes=[pltpu.VMEM((tb, heads * dv), BF16)],
        compiler_params=pltpu.CompilerParams(
            dimension_semantics=("parallel", "arbitrary"), vmem_limit_bytes=VMEM_LIMIT_BYTES),
        name="retention",
    )(proj, x, cos, sin, s0, jnp.asarray(dmat), gn, wout)


def _gla_tables(lc, heads):
    nlev = int(math.log2(lc))
    i = np.arange(lc)[:, None]
    j = np.arange(lc)[None, :]
    masks = []
    for lev in range(nlev):
        h = lc >> (lev + 1)
        same = (i // (2 * h)) == (j // (2 * h))
        masks.append(same & ((i // h) % 2 == 1) & ((j // h) % 2 == 0))
    masks.append(i == j)
    masks = np.tile(np.stack(masks).astype(np.float32), (1, 1, heads))
    tri = (j <= i).astype(np.float32)
    return tri, masks, nlev


def _gla_kernel(proj_ref, x_ref, s0_ref, wa2_ref, ba_ref, tri_ref, masks_ref, ng_ref, wout_ref,
                xo_ref, s_ref, y_scr, *, lc, nlev, heads, dk, dv):
    t = pl.program_id(1)

    @pl.when(t == 0)
    def _():
        s_ref[...] = s0_ref[...]

    tb = proj_ref.shape[0]
    qk = heads * dk
    v_off = 2 * qk
    r_off = v_off + heads * dv
    a_off = r_off + heads * dv
    scale = dk ** -0.5
    row = lax.broadcasted_iota(jnp.int32, (tb, 1), 0)
    zero_k = jnp.zeros((lc, dk), BF16)

    def blockdiag(kk):
        return jnp.concatenate(
            [jnp.concatenate([kk[:, h * dk:(h + 1) * dk] if g == h else zero_k
                              for g in range(heads)], axis=1) for h in range(heads)], axis=0)

    def block_rows(x, period, r):
        return jnp.concatenate(
            [jnp.broadcast_to(x[m:m + 1, :], (period, x.shape[1]))
             for m in range(r, x.shape[0], period)], axis=0)

    def level_exponent(h, b, lg):
        if h == 1:
            return jnp.where((row & 1) == 1, lg, 0.0)
        if h == 2:
            r4 = row & 3
            lg_next = pltpu.roll(lg, tb - 1, 0)
            lg_prev = pltpu.roll(lg, 1, 0)
            return jnp.where(r4 == 0, lg_next,
                             jnp.where(r4 == 1, 0.0, jnp.where(r4 == 2, lg, lg_prev + lg)))
        return -jnp.abs(b - block_rows(b, 2 * h, h - 1))

    nc = tb // lc
    chunks = [slice(c * lc, (c + 1) * lc) for c in range(nc)]
    q = proj_ref[:, 0:qk].astype(F32)
    k = proj_ref[:, qk:2 * qk].astype(F32) * scale
    z = _dot(proj_ref[:, a_off:a_off + LANES], wa2_ref[...]) + ba_ref[...]
    lg = (jnp.minimum(z, 0.0) - jnp.log(1.0 + jnp.exp(-jnp.abs(z)))) * (1.0 / GLA_TAU)
    lg_hi = lg.astype(BF16)
    lg_lo = (lg - lg_hi.astype(F32)).astype(BF16)
    tri = tri_ref[...]
    b = jnp.concatenate([_dot(tri, lg_hi[rs]) + _dot(tri, lg_lo[rs]) for rs in chunks], axis=0)
    b_last = block_rows(b, lc, lc - 1)
    qx = (q * jnp.exp(b)).astype(BF16)
    kx = (k * jnp.exp(b_last - b)).astype(BF16)
    f_last = jnp.exp(b_last)

    qb, kb = q.astype(BF16), k.astype(BF16)
    p = [masks_ref[nlev] * _dot_nt(qb[rs], blockdiag(kb[rs])) for rs in chunks]
    for lev in range(nlev):
        h = lc >> (lev + 1)
        second = ((row // h) & 1) == 1
        m = (jnp.where(second, q, k) * jnp.exp(level_exponent(h, b, lg))).astype(BF16)
        for c, rs in enumerate(chunks):
            p[c] = p[c] + masks_ref[lev] * _dot_nt(m[rs], blockdiag(m[rs]))

    v = [[proj_ref[rs, v_off + h * dv:v_off + (h + 1) * dv] for h in range(heads)]
         for rs in chunks]
    ds = [[_dot_tn(kx[rs, h * dk:(h + 1) * dk], v[c][h]) for h in range(heads)]
          for c, rs in enumerate(chunks)]

    s = [s_ref[h] for h in range(heads)]
    for c, rs in enumerate(chunks):
        p_bf = p[c].astype(BF16)
        for h in range(heads):
            ks = slice(h * dk, (h + 1) * dk)
            g0 = (h * lc) // LANES * LANES
            off = h * lc - g0
            lhs = jnp.concatenate([qx[rs, ks], p_bf[:, g0:g0 + LANES]], axis=1)
            rhs = [s[h].astype(BF16)]
            if off:
                rhs.append(jnp.zeros((off, dv), BF16))
            rhs.append(v[c][h])
            if LANES - off - lc:
                rhs.append(jnp.zeros((LANES - off - lc, dv), BF16))
            o = _dot(lhs, jnp.concatenate(rhs, axis=0))
            fl = f_last[c * lc:c * lc + 1, ks]
            decay = jnp.broadcast_to(fl, (dk, dk)).T
            s[h] = s[h] * jnp.concatenate([decay] * (dv // dk), axis=1) + ds[c][h]
            r = proj_ref[rs, r_off + h * dv:r_off + (h + 1) * dv].astype(F32)
            ms = jnp.mean(o * o, axis=-1, keepdims=True)
            on = o * lax.rsqrt(ms + EPS) * ng_ref[:, h * dv:(h + 1) * dv]
            y_scr[rs, h * dv:(h + 1) * dv] = (_silu(r) * on).astype(BF16)
    for h in range(heads):
        s_ref[h] = s[h]

    xo_ref[...] = x_ref[...] + _dot(y_scr[...], wout_ref[...])


def _gla(proj, x, s0, wa2p, ba, ng, wout, tb, lc):
    b, t, d = x.shape
    heads, dk, dv = s0.shape[1:]
    n = proj.shape[-1]
    tri, masks, nlev = _gla_tables(lc, heads)
    kern = functools.partial(_gla_kernel, lc=lc, nlev=nlev, heads=heads, dk=dk, dv=dv)
    return pl.pallas_call(
        kern,
        grid=(b, t // tb),
        in_specs=[
            pl.BlockSpec((None, tb, n), lambda i, j: (i, j, 0)),
            pl.BlockSpec((None, tb, d), lambda i, j: (i, j, 0)),
            pl.BlockSpec((None, heads, dk, dv), lambda i, j: (i, 0, 0, 0)),
            _const_spec(wa2p.shape),
            _const_spec((1, heads * dk)),
            _const_spec(tri.shape),
            _const_spec(masks.shape),
            _const_spec((1, heads * dv)),
            _const_spec((heads * dv, d)),
        ],
        out_specs=[
            pl.BlockSpec((None, tb, d), lambda i, j: (i, j, 0)),
            pl.BlockSpec((None, heads, dk, dv), lambda i, j: (i, 0, 0, 0)),
        ],
        out_shape=[
            jax.ShapeDtypeStruct((b, t, d), F32),
            jax.ShapeDtypeStruct(s0.shape, F32),
        ],
        scratch_shapes=[pltpu.VMEM((tb, heads * dv), BF16)],
        compiler_params=pltpu.CompilerParams(
            dimension_semantics=("parallel", "arbitrary"), vmem_limit_bytes=VMEM_LIMIT_BYTES),
        name="gla",
    )(proj, x, s0, wa2p, ba, jnp.asarray(tri, BF16), jnp.asarray(masks), ng, wout)


def _ffn_kernel(x_hbm, g_ref, wup_ref, cw_ref, cb_ref, wdn_ref, cs_ref, gf_ref,
                xo_hbm, nc_ref, xbuf, obuf, hbuf, ubuf, abuf, in_sem, out_sem,
                *, cwid, final_norm):
    bi, t = pl.program_id(0), pl.program_id(1)
    nt = pl.num_programs(1)
    nsteps = pl.num_programs(0) * nt
    step = bi * nt + t
    slot = step % 2
    lslab, nslab = xbuf.shape[1], xbuf.shape[2]
    tm = lslab * nslab
    dff = wdn_ref.shape[0]

    def slab_copies(to_vmem, b_idx, t_idx, sl):
        copies = []
        for s_i in range(nslab):
            hbm = (x_hbm if to_vmem else xo_hbm).at[b_idx, pl.ds(t_idx * tm + s_i * lslab, lslab), :]
            if to_vmem:
                copies.append(pltpu.make_async_copy(hbm, xbuf.at[sl, :, s_i, :], in_sem.at[sl, s_i]))
            else:
                copies.append(pltpu.make_async_copy(obuf.at[sl, :, s_i, :], hbm, out_sem.at[sl, s_i]))
        return copies

    @pl.when(step == 0)
    def _():
        for cp in slab_copies(True, bi, t, slot):
            cp.start()

    @pl.when(step + 1 < nsteps)
    def _():
        nxt = step + 1
        for cp in slab_copies(True, nxt // nt, nxt % nt, 1 - slot):
            cp.start()

    @pl.when(t == 0)
    def _():
        nc_ref[...] = cs_ref[...]

    for cp in slab_copies(True, bi, t, slot):
        cp.wait()

    @pl.when(step >= 2)
    def _():
        for cp in slab_copies(False, bi, t, slot):
            cp.wait()

    d = xbuf.shape[-1]
    hbuf[...] = _rms(xbuf[slot].reshape(tm, d), g_ref[...]).astype(BF16)
    hdr = 2 * SUBLANES
    sub = lax.broadcasted_iota(jnp.int32, (SUBLANES, 1), 0)

    def conv(c0, slot):
        cols = slice(c0, c0 + cwid)
        u = _dot(hbuf[...], wup_ref[:, cols])
        ub = ubuf.at[slot]
        ub[0:SUBLANES, :] = jnp.where(sub == 0, nc_ref[0:1, cols],
                                      pltpu.roll(u[tm - 2 * SUBLANES:tm - SUBLANES, :], 1, 0))
        ub[SUBLANES:hdr, :] = jnp.where(sub == 0, nc_ref[1:2, cols],
                                        pltpu.roll(u[tm - SUBLANES:tm, :], 1, 0))
        ub[hdr:hdr + tm, :] = u
        nc_ref[0:1, cols] = u[tm - SUBLANES - 1:tm - SUBLANES, :]
        nc_ref[1:2, cols] = u[tm - 1:tm, :]
        return (cb_ref[:, cols] + ub[0:tm, :] * cw_ref[0:1, cols]
                + ub[SUBLANES:SUBLANES + tm, :] * cw_ref[1:2, cols] + u * cw_ref[2:3, cols])

    for c in range(dff // cwid):
        gate = conv(c * cwid, 2 * (c % 2))
        val = conv(dff + c * cwid, 2 * (c % 2) + 1)
        abuf[:, c * cwid:(c + 1) * cwid] = (_silu(gate) * val).astype(BF16)
    out = xbuf[slot].reshape(tm, d) + _dot(abuf[...], wdn_ref[...])
    if final_norm:
        out = _rms(out, gf_ref[...])
    obuf[slot] = out.reshape(lslab, nslab, d)
    for cp in slab_copies(False, bi, t, slot):
        cp.start()

    @pl.when(step == nsteps - 1)
    def _():
        for cp in slab_copies(False, bi, t, slot):
            cp.wait()

        @pl.when(step >= 1)
        def _():
            for cp in slab_copies(False, bi, t, 1 - slot):
                cp.wait()


def _conv_ffn(x, g, wup, cw, cb, wdn, cs, gf, layer, tm, final_norm, cwid=256):
    b, t, d = x.shape
    dff = wdn.shape[1]
    kern = functools.partial(_ffn_kernel, cwid=cwid, final_norm=final_norm)
    return pl.pallas_call(
        kern,
        grid=(b, t // tm),
        in_specs=[
            pl.BlockSpec(memory_space=pl.ANY),
            _layer_spec((1, d), layer),
            _layer_spec((d, 2 * dff), layer),
            _layer_spec((CONV_W, 2 * dff), layer),
            _layer_spec((1, 2 * dff), layer),
            _layer_spec((dff, d), layer),
            pl.BlockSpec((None, None, CONV_W - 1, 2 * dff), lambda i, j: (layer, i, 0, 0)),
            _const_spec((1, d)),
        ],
        out_specs=[
            pl.BlockSpec(memory_space=pl.ANY),
            pl.BlockSpec((None, CONV_W - 1, 2 * dff), lambda i, j: (i, 0, 0)),
        ],
        out_shape=[
            jax.ShapeDtypeStruct((b, t, d), F32),
            jax.ShapeDtypeStruct((b, CONV_W - 1, 2 * dff), F32),
        ],
        scratch_shapes=[
            pltpu.VMEM((2, tm // SUBLANES, SUBLANES, d), F32),
            pltpu.VMEM((2, tm // SUBLANES, SUBLANES, d), F32),
            pltpu.VMEM((tm, d), BF16),
            pltpu.VMEM((4, tm + 2 * SUBLANES, cwid), F32),
            pltpu.VMEM((tm, dff), BF16),
            pltpu.SemaphoreType.DMA((2, SUBLANES)),
            pltpu.SemaphoreType.DMA((2, SUBLANES)),
        ],
        compiler_params=pltpu.CompilerParams(
            dimension_semantics=("arbitrary", "arbitrary"), vmem_limit_bytes=VMEM_LIMIT_BYTES),
        name="conv_ffn",
    )(x, g, wup, cw, cb, wdn, cs, gf)


def _tile(t, pref):
    return pref if t % pref == 0 else t


def _trunk(x, pos0, ret_state, gla_state, conv_state, w):
    b, t, d = x.shape
    tm = _tile(b * t, 512)
    tb = _tile(t, 512)
    ret_lc = _tile(tb, RET_CHUNK)
    gla_tb = _tile(t, GLA_TILE)
    gla_lc = _tile(gla_tb, GLA_CHUNK)

    half = ret_state.shape[-2] // 2
    inv = ROPE_BASE ** (-jnp.arange(half, dtype=F32) / half)
    blk = min(t, 64)
    ang_a = (pos0 + blk * jnp.arange(t // blk, dtype=jnp.int32)).astype(F32)[:, None] * inv[None, :]
    ang_r = jnp.arange(blk, dtype=F32)[:, None] * inv[None, :]
    ca, sa = jnp.cos(ang_a)[:, None, :], jnp.sin(ang_a)[:, None, :]
    cr, sr = jnp.cos(ang_r)[None, :, :], jnp.sin(ang_r)[None, :, :]
    cos = (ca * cr - sa * sr).reshape(t, half)
    sin = (sa * cr + ca * sr).reshape(t, half)

    proj = _norm_proj(x.reshape(b * t, d), w["norm_mix"][0:1], w["ret_w_in"], tm)
    x, ret_s = _retention(proj.reshape(b, t, -1), x, cos, sin, ret_state, w["ret_gn_g"],
                          w["ret_w_out"], tb, ret_lc)
    ffn = (w["norm_ffn"], w["ffn_w_up"], w["ffn_conv_w"], w["ffn_conv_b"], w["ffn_w_down"],
           conv_state, w["norm_final"])
    x, conv0 = _conv_ffn(x, *ffn, 0, _tile(t, FFN_TILE), False)
    proj = _norm_proj(x.reshape(b * t, d), w["norm_mix"][1:2], w["gla_w_in"], tm)
    x, gla_s = _gla(proj.reshape(b, t, -1), x, gla_state, w["gla_w_a2"], w["gla_b_a"],
                    w["gla_norm_g"], w["gla_w_out"], gla_tb, gla_lc)
    x, conv1 = _conv_ffn(x, *ffn, 1, _tile(t, FFN_TILE), True)
    return x, ret_s[None], gla_s[None], jnp.stack([conv0, conv1])


def kernel(x_prompt, x_sample, state_ret, state_gla, cache_conv, norm_mix, norm_ffn, norm_final,
           ret_w_in, ret_gn_g, ret_w_out, gla_w_in, gla_w_a2, gla_b_a, gla_norm_g, gla_w_out,
           ffn_w_up, ffn_conv_w, ffn_conv_b, ffn_w_down):
    assert norm_mix.shape[0] == 2 and ret_w_in.shape[0] == 1 and gla_w_in.shape[0] == 1
    d = x_prompt.shape[-1]
    gla_qk = gla_w_a2.shape[-1]
    gla_v = gla_w_out.shape[1]
    n_main = 2 * gla_qk + 2 * gla_v
    gla_w_in_p = jnp.concatenate(
        [gla_w_in[0, :, :n_main],
         jnp.pad(gla_w_in[0, :, n_main:], ((0, 0), (0, LANES - GLA_GATE_RANK)))], axis=1)
    w = dict(
        norm_mix=norm_mix, norm_ffn=norm_ffn[:, None, :], norm_final=norm_final.reshape(1, d),
        ret_w_in=ret_w_in[0].astype(BF16),
        ret_gn_g=ret_gn_g[0].reshape(1, -1),
        ret_w_out=ret_w_out[0].astype(BF16),
        gla_w_in=gla_w_in_p.astype(BF16),
        gla_w_a2=jnp.pad(gla_w_a2[0], ((0, LANES - GLA_GATE_RANK), (0, 0))).astype(BF16),
        gla_b_a=gla_b_a[0].reshape(1, -1),
        gla_norm_g=gla_norm_g[0].reshape(1, -1),
        gla_w_out=gla_w_out[0].astype(BF16),
        ffn_w_up=ffn_w_up.astype(BF16), ffn_conv_w=ffn_conv_w, ffn_conv_b=ffn_conv_b[:, None, :],
        ffn_w_down=ffn_w_down.astype(BF16),
    )
    bp = x_prompt.shape[0]
    ret0 = jnp.zeros((bp,) + state_ret.shape[2:], F32)
    gla0 = jnp.zeros((bp,) + state_gla.shape[2:], F32)
    conv0 = jnp.zeros((cache_conv.shape[0], bp) + cache_conv.shape[2:], F32)
    y_p, ret_p, gla_p, conv_p = _trunk(x_prompt, 0, ret0, gla0, conv0, w)
    y_s, ret_s, gla_s, conv_s = _trunk(x_sample, PAST_LEN, state_ret[0], state_gla[0],
                                       cache_conv, w)
    return (y_p, y_s, ret_p, ret_s, gla_p, gla_s, conv_p, conv_s)
```

```python
import functools
import math

import numpy as np
import jax
import jax.numpy as jnp
from jax import lax
from jax.experimental import pallas as pl
from jax.experimental.pallas import tpu as pltpu

EPS = 1e-6
ROPE_BASE = 10000.0
GLA_TAU = 16.0
RET_HEADS = 4
GLA_HEADS = 4
GLA_GATE_RANK = 16
CONV_W = 3
PAST_LEN = 2048
GLA_CHUNK = 64
RET_CHUNK = 256
FFN_TILE = 512
GLA_TILE = 512
LANES = 128
SUBLANES = 8
VMEM_LIMIT_BYTES = 56 * 1024 * 1024

F32 = jnp.float32
BF16 = jnp.bfloat16


def _const_spec(shape):
    nd = len(shape)
    return pl.BlockSpec(shape, lambda *_: (0,) * nd, pipeline_mode=pl.Buffered(1))


def _layer_spec(shape, layer):
    nd = len(shape)
    return pl.BlockSpec((None,) + tuple(shape), lambda *_: (layer,) + (0,) * nd,
                        pipeline_mode=pl.Buffered(1))


def _rms(x, g):
    ms = jnp.mean(x * x, axis=-1, keepdims=True)
    return x * lax.rsqrt(ms + EPS) * g


def _silu(x):
    return x * (1.0 / (1.0 + jnp.exp(-x)))


def _dot(a, b):
    return jnp.dot(a, b, preferred_element_type=F32)


def _dot_nt(a, b):
    return lax.dot_general(a, b, (((1,), (1,)), ((), ())), preferred_element_type=F32)


def _dot_tn(a, b):
    return lax.dot_general(a, b, (((0,), (0,)), ((), ())), preferred_element_type=F32)


def _norm_proj_kernel(x_ref, g_ref, w_ref, o_ref, *, tn):
    h = _rms(x_ref[...], g_ref[...]).astype(BF16)
    n = w_ref.shape[1]
    for n0 in range(0, n, tn):
        n1 = min(n0 + tn, n)
        o_ref[:, n0:n1] = _dot(h, w_ref[:, n0:n1]).astype(o_ref.dtype)


def _norm_proj(x2, g, w, tm, tn=512):
    m, d = x2.shape
    n = w.shape[1]
    return pl.pallas_call(
        functools.partial(_norm_proj_kernel, tn=tn),
        grid=(m // tm,),
        in_specs=[
            pl.BlockSpec((tm, d), lambda i: (i, 0)),
            _const_spec((1, d)),
            _const_spec((d, n)),
        ],
        out_specs=pl.BlockSpec((tm, n), lambda i: (i, 0)),
        out_shape=jax.ShapeDtypeStruct((m, n), BF16),
        compiler_params=pltpu.CompilerParams(
            dimension_semantics=("parallel",), vmem_limit_bytes=VMEM_LIMIT_BYTES),
        name="norm_proj",
    )(x2, g, w)


def _ret_kernel(proj_ref, x_ref, cos_ref, sin_ref, s0_ref, dmat_ref, gn_ref, wout_ref,
                xo_ref, s_ref, y_scr, *, lc, heads, dk, dv, log_gamma):
    t = pl.program_id(1)

    @pl.when(t == 0)
    def _():
        s_ref[...] = s0_ref[...]

    tb = proj_ref.shape[0]
    half = dk // 2
    k_off = heads * dk
    v_off = 2 * heads * dk
    g_off = v_off + heads * dv
    scale = dk ** -0.5
    idx = lax.broadcasted_iota(jnp.int32, (lc, 1), 0).astype(F32)

    def rot(u, cos, sin):
        u1, u2 = u[:, :half], u[:, half:]
        return jnp.concatenate([u1 * cos - u2 * sin, u1 * sin + u2 * cos], axis=-1)

    for c in range(tb // lc):
        r0 = c * lc
        cos = cos_ref[r0:r0 + lc, :]
        sin = sin_ref[r0:r0 + lc, :]
        for h in range(heads):
            lg = log_gamma[h]
            q = rot(proj_ref[r0:r0 + lc, h * dk:(h + 1) * dk].astype(F32), cos, sin)
            k = rot(proj_ref[r0:r0 + lc, k_off + h * dk:k_off + (h + 1) * dk].astype(F32),
                    cos, sin) * scale
            v = proj_ref[r0:r0 + lc, v_off + h * dv:v_off + (h + 1) * dv]
            g = proj_ref[r0:r0 + lc, g_off + h * dv:g_off + (h + 1) * dv].astype(F32)
            s_old = s_ref[h]
            scores = _dot_nt(q.astype(BF16), k.astype(BF16)) * dmat_ref[h]
            o = _dot(scores.astype(BF16), v)
            q_dec = jnp.exp(lg * (idx + 1.0))
            o = o + _dot((q * q_dec).astype(BF16), s_old.astype(BF16))
            k_dec = jnp.exp(lg * (lc - 1.0 - idx))
            s_ref[h] = math.exp(lg * lc) * s_old + _dot_tn((k * k_dec).astype(BF16), v)
            mu = jnp.mean(o, axis=-1, keepdims=True)
            d = o - mu
            var = jnp.mean(d * d, axis=-1, keepdims=True)
            on = d * lax.rsqrt(var + EPS) * gn_ref[:, h * dv:(h + 1) * dv]
            y_scr[r0:r0 + lc, h * dv:(h + 1) * dv] = (_silu(g) * on).astype(BF16)

    xo_ref[...] = x_ref[...] + _dot(y_scr[...], wout_ref[...])


def _retention(proj, x, cos, sin, s0, gn, wout, tb, lc):
    b, t, d = x.shape
    heads, dk, dv = s0.shape[1:]
    n = proj.shape[-1]
    log_gamma = [math.log1p(-(2.0 ** (-5.0 - h))) for h in range(heads)]
    ii = np.arange(lc, dtype=np.float64)
    diff = ii[:, None] - ii[None, :]
    dmat = np.stack([np.where(diff >= 0, np.exp(lg * np.maximum(diff, 0.0)), 0.0)
                     for lg in log_gamma]).astype(np.float32)
    kern = functools.partial(_ret_kernel, lc=lc, heads=heads, dk=dk, dv=dv,
                             log_gamma=tuple(log_gamma))
    return pl.pallas_call(
        kern,
        grid=(b, t // tb),
        in_specs=[
            pl.BlockSpec((None, tb, n), lambda i, j: (i, j, 0)),
            pl.BlockSpec((None, tb, d), lambda i, j: (i, j, 0)),
            pl.BlockSpec((tb, dk // 2), lambda i, j: (j, 0)),
            pl.BlockSpec((tb, dk // 2), lambda i, j: (j, 0)),
            pl.BlockSpec((None, heads, dk, dv), lambda i, j: (i, 0, 0, 0)),
            _const_spec((heads, lc, lc)),
            _const_spec((1, heads * dv)),
            _const_spec((heads * dv, d)),
        ],
        out_specs=[
            pl.BlockSpec((None, tb, d), lambda i, j: (i, j, 0)),
            pl.BlockSpec((None, heads, dk, dv), lambda i, j: (i, 0, 0, 0)),
        ],
        out_shape=[
            jax.ShapeDtypeStruct((b, t, d), F32),
            jax.ShapeDtypeStruct(s0.shape, F32),
        ],
        scratch_shapes=[pltpu.VMEM((tb, heads * dv), BF16)],
        compiler_params=pltpu.CompilerParams(
            dimension_semantics=("parallel", "arbitrary"), vmem_limit_bytes=VMEM_LIMIT_BYTES),
        name="retention",
    )(proj, x, cos, sin, s0, jnp.asarray(dmat), gn, wout)


def _gla_tables(lc, heads):
    nlev = int(math.log2(lc))
    i = np.arange(lc)[:, None]
    j = np.arange(lc)[None, :]
    masks = []
    for lev in range(nlev):
        h = lc >> (lev + 1)
        same = (i // (2 * h)) == (j // (2 * h))
        masks.append(same & ((i // h) % 2 == 1) & ((j // h) % 2 == 0))
    masks.append(i == j)
    masks = np.tile(np.stack(masks).astype(np.float32), (1, 1, heads))
    tri = (j <= i).astype(np.float32)
    return tri, masks, nlev


def _gla_kernel(proj_ref, x_ref, s0_ref, wa2_ref, ba_ref, tri_ref, masks_ref, ng_ref, wout_ref,
                xo_ref, s_ref, y_scr, *, lc, nlev, heads, dk, dv):
    t = pl.program_id(1)

    @pl.when(t == 0)
    def _():
        s_ref[...] = s0_ref[...]

    tb = proj_ref.shape[0]
    qk = heads * dk
    v_off = 2 * qk
    r_off = v_off + heads * dv
    a_off = r_off + heads * dv
    scale = dk ** -0.5
    row = lax.broadcasted_iota(jnp.int32, (tb, 1), 0)
    zero_k = jnp.zeros((lc, dk), BF16)

    def blockdiag(kk):
        return jnp.concatenate(
            [jnp.concatenate([kk[:, h * dk:(h + 1) * dk] if g == h else zero_k
                              for g in range(heads)], axis=1) for h in range(heads)], axis=0)

    def block_rows(x, period, r):
        return jnp.concatenate(
            [jnp.broadcast_to(x[m:m + 1, :], (period, x.shape[1]))
             for m in range(r, x.shape[0], period)], axis=0)

    def level_exponent(h, b, lg):
        if h == 1:
            return jnp.where((row & 1) == 1, lg, 0.0)
        if h == 2:
            r4 = row & 3
            lg_next = pltpu.roll(lg, tb - 1, 0)
            lg_prev = pltpu.roll(lg, 1, 0)
            return jnp.where(r4 == 0, lg_next,
                             jnp.where(r4 == 1, 0.0, jnp.where(r4 == 2, lg, lg_prev + lg)))
        return -jnp.abs(b - block_rows(b, 2 * h, h - 1))

    nc = tb // lc
    chunks = [slice(c * lc, (c + 1) * lc) for c in range(nc)]
    q = proj_ref[:, 0:qk].astype(F32)
    k = proj_ref[:, qk:2 * qk].astype(F32) * scale
    z = _dot(proj_ref[:, a_off:a_off + LANES], wa2_ref[...]) + ba_ref[...]
    lg = (jnp.minimum(z, 0.0) - jnp.log(1.0 + jnp.exp(-jnp.abs(z)))) * (1.0 / GLA_TAU)
    lg_hi = lg.astype(BF16)
    lg_lo = (lg - lg_hi.astype(F32)).astype(BF16)
    tri = tri_ref[...]
    b = jnp.concatenate([_dot(tri, lg_hi[rs]) + _dot(tri, lg_lo[rs]) for rs in chunks], axis=0)
    b_last = block_rows(b, lc, lc - 1)
    qx = (q * jnp.exp(b)).astype(BF16)
    kx = (k * jnp.exp(b_last - b)).astype(BF16)
    f_last = jnp.exp(b_last)

    qb, kb = q.astype(BF16), k.astype(BF16)
    p = [masks_ref[nlev] * _dot_nt(qb[rs], blockdiag(kb[rs])) for rs in chunks]
    for lev in range(nlev):
        h = lc >> (lev + 1)
        second = ((row // h) & 1) == 1
        m = (jnp.where(second, q, k) * jnp.exp(level_exponent(h, b, lg))).astype(BF16)
        for c, rs in enumerate(chunks):
            p[c] = p[c] + masks_ref[lev] * _dot_nt(m[rs], blockdiag(m[rs]))

    v = [[proj_ref[rs, v_off + h * dv:v_off + (h + 1) * dv] for h in range(heads)]
         for rs in chunks]
    ds = [[_dot_tn(kx[rs, h * dk:(h + 1) * dk], v[c][h]) for h in range(heads)]
          for c, rs in enumerate(chunks)]

    s = [s_ref[h] for h in range(heads)]
    for c, rs in enumerate(chunks):
        p_bf = p[c].astype(BF16)
        for h in range(heads):
            ks = slice(h * dk, (h + 1) * dk)
            g0 = (h * lc) // LANES * LANES
            off = h * lc - g0
            lhs = jnp.concatenate([qx[rs, ks], p_bf[:, g0:g0 + LANES]], axis=1)
            rhs = [s[h].astype(BF16)]
            if off:
                rhs.append(jnp.zeros((off, dv), BF16))
            rhs.append(v[c][h])
            if LANES - off - lc:
                rhs.append(jnp.zeros((LANES - off - lc, dv), BF16))
            o = _dot(lhs, jnp.concatenate(rhs, axis=0))
            fl = f_last[c * lc:c * lc + 1, ks]
            decay = jnp.broadcast_to(fl, (dk, dk)).T
            s[h] = s[h] * jnp.concatenate([decay] * (dv // dk), axis=1) + ds[c][h]
            r = proj_ref[rs, r_off + h * dv:r_off + (h + 1) * dv].astype(F32)
            ms = jnp.mean(o * o, axis=-1, keepdims=True)
            on = o * lax.rsqrt(ms + EPS) * ng_ref[:, h * dv:(h + 1) * dv]
            y_scr[rs, h * dv:(h + 1) * dv] = (_silu(r) * on).astype(BF16)
    for h in range(heads):
        s_ref[h] = s[h]

    xo_ref[...] = x_ref[...] + _dot(y_scr[...], wout_ref[...])


def _gla(proj, x, s0, wa2p, ba, ng, wout, tb, lc):
    b, t, d = x.shape
    heads, dk, dv = s0.shape[1:]
    n = proj.shape[-1]
    tri, masks, nlev = _gla_tables(lc, heads)
    kern = functools.partial(_gla_kernel, lc=lc, nlev=nlev, heads=heads, dk=dk, dv=dv)
    return pl.pallas_call(
        kern,
        grid=(b, t // tb),
        in_specs=[
            pl.BlockSpec((None, tb, n), lambda i, j: (i, j, 0)),
            pl.BlockSpec((None, tb, d), lambda i, j: (i, j, 0)),
            pl.BlockSpec((None, heads, dk, dv), lambda i, j: (i, 0, 0, 0)),
            _const_spec(wa2p.shape),
            _const_spec((1, heads * dk)),
            _const_spec(tri.shape),
            _const_spec(masks.shape),
            _const_spec((1, heads * dv)),
            _const_spec((heads * dv, d)),
        ],
        out_specs=[
            pl.BlockSpec((None, tb, d), lambda i, j: (i, j, 0)),
            pl.BlockSpec((None, heads, dk, dv), lambda i, j: (i, 0, 0, 0)),
        ],
        out_shape=[
            jax.ShapeDtypeStruct((b, t, d), F32),
            jax.ShapeDtypeStruct(s0.shape, F32),
        ],
        scratch_shapes=[pltpu.VMEM((tb, heads * dv), BF16)],
        compiler_params=pltpu.CompilerParams(
            dimension_semantics=("parallel", "arbitrary"), vmem_limit_bytes=VMEM_LIMIT_BYTES),
        name="gla",
    )(proj, x, s0, wa2p, ba, jnp.asarray(tri, BF16), jnp.asarray(masks), ng, wout)


def _ffn_kernel(x_hbm, g_ref, wup_ref, cw_ref, cb_ref, wdn_ref, cs_ref, gf_ref,
                xo_hbm, nc_ref, xbuf, obuf, hbuf, ubuf, abuf, in_sem, out_sem,
                *, cwid, final_norm, seq_slabs):
    bi, t = pl.program_id(0), pl.program_id(1)
    nt = pl.num_programs(1)
    nsteps = pl.num_programs(0) * nt
    step = bi * nt + t
    slot = step % 2
    lslab, nslab = xbuf.shape[1], xbuf.shape[2]
    tm = lslab * nslab
    dff = wdn_ref.shape[0]

    def slab_copies(to_vmem, b_idx, t_idx, sl):
        copies = []
        for s_i in range(nslab):
            ref = x_hbm if to_vmem else xo_hbm
            if seq_slabs:
                hbm = ref.at[s_i]
            else:
                hbm = ref.at[b_idx, pl.ds(t_idx * tm + s_i * lslab, lslab), :]
            if to_vmem:
                copies.append(pltpu.make_async_copy(hbm, xbuf.at[sl, :, s_i, :], in_sem.at[sl, s_i]))
            else:
                copies.append(pltpu.make_async_copy(obuf.at[sl, :, s_i, :], hbm, out_sem.at[sl, s_i]))
        return copies

    @pl.when(step == 0)
    def _():
        for cp in slab_copies(True, bi, t, slot):
            cp.start()

    @pl.when(step + 1 < nsteps)
    def _():
        nxt = step + 1
        for cp in slab_copies(True, nxt // nt, nxt % nt, 1 - slot):
            cp.start()

    if not seq_slabs:
        @pl.when(t == 0)
        def _():
            nc_ref[...] = cs_ref[...]

    for cp in slab_copies(True, bi, t, slot):
        cp.wait()

    @pl.when(step >= 2)
    def _():
        for cp in slab_copies(False, bi, t, slot):
            cp.wait()

    d = xbuf.shape[-1]
    hbuf[...] = _rms(xbuf[slot].reshape(tm, d), g_ref[...]).astype(BF16)
    hdr = 2 * SUBLANES
    sub = lax.broadcasted_iota(jnp.int32, (SUBLANES, 1), 0)

    def conv(c0, slot):
        cols = slice(c0, c0 + cwid)
        u = _dot(hbuf[...], wup_ref[:, cols])
        ub = ubuf.at[slot]
        if seq_slabs:
            for r in range(CONV_W - 1):
                ub[r * SUBLANES:(r + 1) * SUBLANES, :] = jnp.concatenate(
                    [cs_ref[s_i, r:r + 1, cols] for s_i in range(nslab)], axis=0)
                for s_i in range(nslab):
                    row = tm - (CONV_W - 1 - r) * SUBLANES + s_i
                    nc_ref[s_i, r:r + 1, cols] = u[row:row + 1, :]
        else:
            ub[0:SUBLANES, :] = jnp.where(sub == 0, nc_ref[0:1, cols],
                                          pltpu.roll(u[tm - 2 * SUBLANES:tm - SUBLANES, :], 1, 0))
            ub[SUBLANES:hdr, :] = jnp.where(sub == 0, nc_ref[1:2, cols],
                                            pltpu.roll(u[tm - SUBLANES:tm, :], 1, 0))
            nc_ref[0:1, cols] = u[tm - SUBLANES - 1:tm - SUBLANES, :]
            nc_ref[1:2, cols] = u[tm - 1:tm, :]
        ub[hdr:hdr + tm, :] = u
        return (cb_ref[:, cols] + ub[0:tm, :] * cw_ref[0:1, cols]
                + ub[SUBLANES:SUBLANES + tm, :] * cw_ref[1:2, cols] + u * cw_ref[2:3, cols])

    for c in range(dff // cwid):
        gate = conv(c * cwid, 2 * (c % 2))
        val = conv(dff + c * cwid, 2 * (c % 2) + 1)
        abuf[:, c * cwid:(c + 1) * cwid] = (_silu(gate) * val).astype(BF16)
    out = xbuf[slot].reshape(tm, d) + _dot(abuf[...], wdn_ref[...])
    if final_norm:
        out = _rms(out, gf_ref[...])
    obuf[slot] = out.reshape(lslab, nslab, d)
    for cp in slab_copies(False, bi, t, slot):
        cp.start()

    @pl.when(step == nsteps - 1)
    def _():
        for cp in slab_copies(False, bi, t, slot):
            cp.wait()

        @pl.when(step >= 1)
        def _():
            for cp in slab_copies(False, bi, t, 1 - slot):
                cp.wait()


def _conv_ffn(x, g, wup, cw, cb, wdn, cs, gf, layer, tm, final_norm, cwid=256):
    b, t, d = x.shape
    dff = wdn.shape[1]
    seq_slabs = b == SUBLANES and t < FFN_TILE
    if seq_slabs:
        tm = b * t
        grid = (1, 1)
        cs_spec = pl.BlockSpec((None, b, CONV_W - 1, 2 * dff), lambda i, j: (layer, 0, 0, 0))
        nc_spec = pl.BlockSpec((b, CONV_W - 1, 2 * dff), lambda i, j: (0, 0, 0))
    else:
        grid = (b, t // tm)
        cs_spec = pl.BlockSpec((None, None, CONV_W - 1, 2 * dff), lambda i, j: (layer, i, 0, 0))
        nc_spec = pl.BlockSpec((None, CONV_W - 1, 2 * dff), lambda i, j: (i, 0, 0))
    kern = functools.partial(_ffn_kernel, cwid=cwid, final_norm=final_norm, seq_slabs=seq_slabs)
    return pl.pallas_call(
        kern,
        grid=grid,
        in_specs=[
            pl.BlockSpec(memory_space=pl.ANY),
            _layer_spec((1, d), layer),
            _layer_spec((d, 2 * dff), layer),
            _layer_spec((CONV_W, 2 * dff), layer),
            _layer_spec((1, 2 * dff), layer),
            _layer_spec((dff, d), layer),
            cs_spec,
            _const_spec((1, d)),
        ],
        out_specs=[pl.BlockSpec(memory_space=pl.ANY), nc_spec],
        out_shape=[
            jax.ShapeDtypeStruct((b, t, d), F32),
            jax.ShapeDtypeStruct((b, CONV_W - 1, 2 * dff), F32),
        ],
        scratch_shapes=[
            pltpu.VMEM((2, tm // SUBLANES, SUBLANES, d), F32),
            pltpu.VMEM((2, tm // SUBLANES, SUBLANES, d), F32),
            pltpu.VMEM((tm, d), BF16),
            pltpu.VMEM((4, tm + 2 * SUBLANES, cwid), F32),
            pltpu.VMEM((tm, dff), BF16),
            pltpu.SemaphoreType.DMA((2, SUBLANES)),
            pltpu.SemaphoreType.DMA((2, SUBLANES)),
        ],
        compiler_params=pltpu.CompilerParams(
            dimension_semantics=("arbitrary", "arbitrary"), vmem_limit_bytes=VMEM_LIMIT_BYTES),
        name="conv_ffn",
    )(x, g, wup, cw, cb, wdn, cs, gf)


def _tile(t, pref):
    return pref if t % pref == 0 else t


def _trunk(x, pos0, ret_state, gla_state, conv_state, w):
    b, t, d = x.shape
    tm = _tile(b * t, 512)
    tb = _tile(t, 512)
    ret_lc = _tile(tb, RET_CHUNK)
    gla_tb = _tile(t, GLA_TILE)
    gla_lc = _tile(gla_tb, GLA_CHUNK)

    half = ret_state.shape[-2] // 2
    inv = ROPE_BASE ** (-jnp.arange(half, dtype=F32) / half)
    blk = min(t, 64)
    ang_a = (pos0 + blk * jnp.arange(t // blk, dtype=jnp.int32)).astype(F32)[:, None] * inv[None, :]
    ang_r = jnp.arange(blk, dtype=F32)[:, None] * inv[None, :]
    ca, sa = jnp.cos(ang_a)[:, None, :], jnp.sin(ang_a)[:, None, :]
    cr, sr = jnp.cos(ang_r)[None, :, :], jnp.sin(ang_r)[None, :, :]
    cos = (ca * cr - sa * sr).reshape(t, half)
    sin = (sa * cr + ca * sr).reshape(t, half)

    proj = _norm_proj(x.reshape(b * t, d), w["norm_mix"][0:1], w["ret_w_in"], tm)
    x, ret_s = _retention(proj.reshape(b, t, -1), x, cos, sin, ret_state, w["ret_gn_g"],
                          w["ret_w_out"], tb, ret_lc)
    ffn = (w["norm_ffn"], w["ffn_w_up"], w["ffn_conv_w"], w["ffn_conv_b"], w["ffn_w_down"],
           conv_state, w["norm_final"])
    x, conv0 = _conv_ffn(x, *ffn, 0, _tile(t, FFN_TILE), False)
    proj = _norm_proj(x.reshape(b * t, d), w["norm_mix"][1:2], w["gla_w_in"], tm)
    x, gla_s = _gla(proj.reshape(b, t, -1), x, gla_state, w["gla_w_a2"], w["gla_b_a"],
                    w["gla_norm_g"], w["gla_w_out"], gla_tb, gla_lc)
    x, conv1 = _conv_ffn(x, *ffn, 1, _tile(t, FFN_TILE), True)
    return x, ret_s[None], gla_s[None], jnp.stack([conv0, conv1])


def kernel(x_prompt, x_sample, state_ret, state_gla, cache_conv, norm_mix, norm_ffn, norm_final,
           ret_w_in, ret_gn_g, ret_w_out, gla_w_in, gla_w_a2, gla_b_a, gla_norm_g, gla_w_out,
           ffn_w_up, ffn_conv_w, ffn_conv_b, ffn_w_down):
    assert norm_mix.shape[0] == 2 and ret_w_in.shape[0] == 1 and gla_w_in.shape[0] == 1
    d = x_prompt.shape[-1]
    gla_qk = gla_w_a2.shape[-1]
    gla_v = gla_w_out.shape[1]
    n_main = 2 * gla_qk + 2 * gla_v
    gla_w_in_p = jnp.concatenate(
        [gla_w_in[0, :, :n_main],
         jnp.pad(gla_w_in[0, :, n_main:], ((0, 0), (0, LANES - GLA_GATE_RANK)))], axis=1)
    w = dict(
        norm_mix=norm_mix, norm_ffn=norm_ffn[:, None, :], norm_final=norm_final.reshape(1, d),
        ret_w_in=ret_w_in[0].astype(BF16),
        ret_gn_g=ret_gn_g[0].reshape(1, -1),
        ret_w_out=ret_w_out[0].astype(BF16),
        gla_w_in=gla_w_in_p.astype(BF16),
        gla_w_a2=jnp.pad(gla_w_a2[0], ((0, LANES - GLA_GATE_RANK), (0, 0))).astype(BF16),
        gla_b_a=gla_b_a[0].reshape(1, -1),
        gla_norm_g=gla_norm_g[0].reshape(1, -1),
        gla_w_out=gla_w_out[0].astype(BF16),
        ffn_w_up=ffn_w_up.astype(BF16), ffn_conv_w=ffn_conv_w, ffn_conv_b=ffn_conv_b[:, None, :],
        ffn_w_down=ffn_w_down.astype(BF16),
    )
    bp = x_prompt.shape[0]
    ret0 = jnp.zeros((bp,) + state_ret.shape[2:], F32)
    gla0 = jnp.zeros((bp,) + state_gla.shape[2:], F32)
    conv0 = jnp.zeros((cache_conv.shape[0], bp) + cache_conv.shape[2:], F32)
    y_p, ret_p, gla_p, conv_p = _trunk(x_prompt, 0, ret0, gla0, conv0, w)
    y_s, ret_s, gla_s, conv_s = _trunk(x_sample, PAST_LEN, state_ret[0], state_gla[0],
                                       cache_conv, w)
    return (y_p, y_s, ret_p, ret_s, gla_p, gla_s, conv_p, conv_s)
```

```python
import functools
import math

import numpy as np
import jax
import jax.numpy as jnp
from jax import lax
from jax.experimental import pallas as pl
from jax.experimental.pallas import tpu as pltpu

EPS = 1e-6
ROPE_BASE = 10000.0
GLA_TAU = 16.0
GLA_GATE_RANK = 16
CONV_W = 3
PAST_LEN = 2048
GLA_CHUNK = 64
RET_CHUNK = 256
PROJ_TILE = 512
RET_TILE = 512
GLA_TILE = 512
FFN_TILE = 512
LANES = 128
SUBLANES = 8
VMEM_LIMIT_BYTES = 56 * 1024 * 1024

F32 = jnp.float32
BF16 = jnp.bfloat16


def _const_spec(shape):
    nd = len(shape)
    return pl.BlockSpec(shape, lambda *_: (0,) * nd, pipeline_mode=pl.Buffered(1))


def _layer_spec(shape, layer):
    nd = len(shape)
    return pl.BlockSpec((None,) + tuple(shape), lambda *_: (layer,) + (0,) * nd,
                        pipeline_mode=pl.Buffered(1))


def _rms(x, g):
    ms = jnp.mean(x * x, axis=-1, keepdims=True)
    return x * lax.rsqrt(ms + EPS) * g


def _silu(x):
    return x * (1.0 / (1.0 + jnp.exp(-x)))


def _dot(a, b):
    return jnp.dot(a, b, preferred_element_type=F32)


def _dot_nt(a, b):
    return lax.dot_general(a, b, (((1,), (1,)), ((), ())), preferred_element_type=F32)


def _dot_tn(a, b):
    return lax.dot_general(a, b, (((0,), (0,)), ((), ())), preferred_element_type=F32)


def _norm_proj_kernel(x_ref, g_ref, w_ref, o_ref, *, tn):
    h = _rms(x_ref[...], g_ref[...]).astype(BF16)
    n = w_ref.shape[1]
    for n0 in range(0, n, tn):
        n1 = min(n0 + tn, n)
        o_ref[:, n0:n1] = _dot(h, w_ref[:, n0:n1]).astype(o_ref.dtype)


def _norm_proj(x2, g, w, tm, tn=512):
    m, d = x2.shape
    n = w.shape[1]
    return pl.pallas_call(
        functools.partial(_norm_proj_kernel, tn=tn),
        grid=(m // tm,),
        in_specs=[
            pl.BlockSpec((tm, d), lambda i: (i, 0)),
            _const_spec((1, d)),
            _const_spec((d, n)),
        ],
        out_specs=pl.BlockSpec((tm, n), lambda i: (i, 0)),
        out_shape=jax.ShapeDtypeStruct((m, n), BF16),
        compiler_params=pltpu.CompilerParams(
            dimension_semantics=("parallel",), vmem_limit_bytes=VMEM_LIMIT_BYTES),
        name="norm_proj",
    )(x2, g, w)


def _ret_kernel(proj_ref, x_ref, cos_ref, sin_ref, s0_ref, dmat_ref, gn_ref, wout_ref,
                xo_ref, s_ref, y_scr, *, lc, heads, dk, dv, log_gamma):
    t = pl.program_id(1)

    @pl.when(t == 0)
    def _():
        s_ref[...] = s0_ref[...]

    tb = proj_ref.shape[0]
    half = dk // 2
    k_off = heads * dk
    v_off = 2 * heads * dk
    g_off = v_off + heads * dv
    scale = dk ** -0.5
    idx = lax.broadcasted_iota(jnp.int32, (lc, 1), 0).astype(F32)

    def rot(u, cos, sin):
        u1, u2 = u[:, :half], u[:, half:]
        return jnp.concatenate([u1 * cos - u2 * sin, u1 * sin + u2 * cos], axis=-1)

    for c in range(tb // lc):
        r0 = c * lc
        cos = cos_ref[r0:r0 + lc, :]
        sin = sin_ref[r0:r0 + lc, :]
        for h in range(heads):
            lg = log_gamma[h]
            q = rot(proj_ref[r0:r0 + lc, h * dk:(h + 1) * dk].astype(F32), cos, sin)
            k = rot(proj_ref[r0:r0 + lc, k_off + h * dk:k_off + (h + 1) * dk].astype(F32),
                    cos, sin) * scale
            v = proj_ref[r0:r0 + lc, v_off + h * dv:v_off + (h + 1) * dv]
            g = proj_ref[r0:r0 + lc, g_off + h * dv:g_off + (h + 1) * dv].astype(F32)
            s_old = s_ref[h]
            scores = _dot_nt(q.astype(BF16), k.astype(BF16)) * dmat_ref[h]
            o = _dot(scores.astype(BF16), v)
            q_dec = jnp.exp(lg * (idx + 1.0))
            o = o + _dot((q * q_dec).astype(BF16), s_old.astype(BF16))
            k_dec = jnp.exp(lg * (lc - 1.0 - idx))
            s_ref[h] = math.exp(lg * lc) * s_old + _dot_tn((k * k_dec).astype(BF16), v)
            mu = jnp.mean(o, axis=-1, keepdims=True)
            d = o - mu
            var = jnp.mean(d * d, axis=-1, keepdims=True)
            on = d * lax.rsqrt(var + EPS) * gn_ref[:, h * dv:(h + 1) * dv]
            y_scr[r0:r0 + lc, h * dv:(h + 1) * dv] = (_silu(g) * on).astype(BF16)

    xo_ref[...] = x_ref[...] + _dot(y_scr[...], wout_ref[...])


def _retention(proj, x, cos, sin, s0, gn, wout, tb, lc):
    b, t, d = x.shape
    heads, dk, dv = s0.shape[1:]
    n = proj.shape[-1]
    log_gamma = [math.log1p(-(2.0 ** (-5.0 - h))) for h in range(heads)]
    ii = np.arange(lc, dtype=np.float64)
    diff = ii[:, None] - ii[None, :]
    dmat = np.stack([np.where(diff >= 0, np.exp(lg * np.maximum(diff, 0.0)), 0.0)
                     for lg in log_gamma]).astype(np.float32)
    kern = functools.partial(_ret_kernel, lc=lc, heads=heads, dk=dk, dv=dv,
                             log_gamma=tuple(log_gamma))
    return pl.pallas_call(
        kern,
        grid=(b, t // tb),
        in_specs=[
            pl.BlockSpec((None, tb, n), lambda i, j: (i, j, 0)),
            pl.BlockSpec((None, tb, d), lambda i, j: (i, j, 0)),
            pl.BlockSpec((tb, dk // 2), lambda i, j: (j, 0)),
            pl.BlockSpec((tb, dk // 2), lambda i, j: (j, 0)),
            pl.BlockSpec((None, heads, dk, dv), lambda i, j: (i, 0, 0, 0)),
            _const_spec((heads, lc, lc)),
            _const_spec((1, heads * dv)),
            _const_spec((heads * dv, d)),
        ],
        out_specs=[
            pl.BlockSpec((None, tb, d), lambda i, j: (i, j, 0)),
            pl.BlockSpec((None, heads, dk, dv), lambda i, j: (i, 0, 0, 0)),
        ],
        out_shape=[
            jax.ShapeDtypeStruct((b, t, d), F32),
            jax.ShapeDtypeStruct(s0.shape, F32),
        ],
        scratch_shapes=[pltpu.VMEM((tb, heads * dv), BF16)],
        compiler_params=pltpu.CompilerParams(
            dimension_semantics=("parallel", "arbitrary"), vmem_limit_bytes=VMEM_LIMIT_BYTES),
        name="retention",
    )(proj, x, cos, sin, s0, jnp.asarray(dmat), gn, wout)


def _gla_tables(lc, heads):
    nlev = int(math.log2(lc))
    i = np.arange(lc)[:, None]
    j = np.arange(lc)[None, :]
    masks = []
    for lev in range(nlev):
        h = lc >> (lev + 1)
        same = (i // (2 * h)) == (j // (2 * h))
        masks.append(same & ((i // h) % 2 == 1) & ((j // h) % 2 == 0))
    masks.append(i == j)
    masks = np.tile(np.stack(masks).astype(np.float32), (1, 1, heads))
    tri = (j <= i).astype(np.float32)
    return tri, masks, nlev


def _gla_kernel(proj_ref, x_ref, s0_ref, wa2_ref, ba_ref, tri_ref, masks_ref, ng_ref, wout_ref,
                xo_ref, s_ref, y_scr, *, lc, nlev, heads, dk, dv):
    t = pl.program_id(1)

    @pl.when(t == 0)
    def _():
        s_ref[...] = s0_ref[...]

    tb = proj_ref.shape[0]
    qk = heads * dk
    v_off = 2 * qk
    r_off = v_off + heads * dv
    a_off = r_off + heads * dv
    scale = dk ** -0.5
    row = lax.broadcasted_iota(jnp.int32, (tb, 1), 0)
    zero_k = jnp.zeros((lc, dk), BF16)

    def blockdiag(kk):
        return jnp.concatenate(
            [jnp.concatenate([kk[:, h * dk:(h + 1) * dk] if g == h else zero_k
                              for g in range(heads)], axis=1) for h in range(heads)], axis=0)

    def block_rows(x, period, r):
        return jnp.concatenate(
            [jnp.broadcast_to(x[m:m + 1, :], (period, x.shape[1]))
             for m in range(r, x.shape[0], period)], axis=0)

    def level_exponent(h, b, lg):
        if h == 1:
            return jnp.where((row & 1) == 1, lg, 0.0)
        if h == 2:
            r4 = row & 3
            lg_next = pltpu.roll(lg, tb - 1, 0)
            lg_prev = pltpu.roll(lg, 1, 0)
            return jnp.where(r4 == 0, lg_next,
                             jnp.where(r4 == 1, 0.0, jnp.where(r4 == 2, lg, lg_prev + lg)))
        return -jnp.abs(b - block_rows(b, 2 * h, h - 1))

    nc = tb // lc
    chunks = [slice(c * lc, (c + 1) * lc) for c in range(nc)]
    q = proj_ref[:, 0:qk].astype(F32)
    k = proj_ref[:, qk:2 * qk].astype(F32) * scale
    z = _dot(proj_ref[:, a_off:a_off + LANES], wa2_ref[...]) + ba_ref[...]
    lg = (jnp.minimum(z, 0.0) - jnp.log(1.0 + jnp.exp(-jnp.abs(z)))) * (1.0 / GLA_TAU)
    lg_hi = lg.astype(BF16)
    lg_lo = (lg - lg_hi.astype(F32)).astype(BF16)
    tri = tri_ref[...]
    b = jnp.concatenate([_dot(tri, lg_hi[rs]) + _dot(tri, lg_lo[rs]) for rs in chunks], axis=0)
    b_last = block_rows(b, lc, lc - 1)
    qx = (q * jnp.exp(b)).astype(BF16)
    kx = (k * jnp.exp(b_last - b)).astype(BF16)
    f_last = jnp.exp(b_last)

    qb, kb = q.astype(BF16), k.astype(BF16)
    p = [masks_ref[nlev] * _dot_nt(qb[rs], blockdiag(kb[rs])) for rs in chunks]
    for lev in range(nlev):
        h = lc >> (lev + 1)
        second = ((row // h) & 1) == 1
        m = (jnp.where(second, q, k) * jnp.exp(level_exponent(h, b, lg))).astype(BF16)
        for c, rs in enumerate(chunks):
            p[c] = p[c] + masks_ref[lev] * _dot_nt(m[rs], blockdiag(m[rs]))

    v = [[proj_ref[rs, v_off + h * dv:v_off + (h + 1) * dv] for h in range(heads)]
         for rs in chunks]
    ds = [[_dot_tn(kx[rs, h * dk:(h + 1) * dk], v[c][h]) for h in range(heads)]
          for c, rs in enumerate(chunks)]

    s = [s_ref[h] for h in range(heads)]
    for c, rs in enumerate(chunks):
        p_bf = p[c].astype(BF16)
        for h in range(heads):
            ks = slice(h * dk, (h + 1) * dk)
            g0 = (h * lc) // LANES * LANES
            off = h * lc - g0
            lhs = jnp.concatenate([qx[rs, ks], p_bf[:, g0:g0 + LANES]], axis=1)
            rhs = [s[h].astype(BF16)]
            if off:
                rhs.append(jnp.zeros((off, dv), BF16))
            rhs.append(v[c][h])
            if LANES - off - lc:
                rhs.append(jnp.zeros((LANES - off - lc, dv), BF16))
            o = _dot(lhs, jnp.concatenate(rhs, axis=0))
            fl = f_last[c * lc:c * lc + 1, ks]
            decay = jnp.broadcast_to(fl, (dk, dk)).T
            s[h] = s[h] * jnp.concatenate([decay] * (dv // dk), axis=1) + ds[c][h]
            r = proj_ref[rs, r_off + h * dv:r_off + (h + 1) * dv].astype(F32)
            ms = jnp.mean(o * o, axis=-1, keepdims=True)
            on = o * lax.rsqrt(ms + EPS) * ng_ref[:, h * dv:(h + 1) * dv]
            y_scr[rs, h * dv:(h + 1) * dv] = (_silu(r) * on).astype(BF16)
    for h in range(heads):
        s_ref[h] = s[h]

    xo_ref[...] = x_ref[...] + _dot(y_scr[...], wout_ref[...])


def _gla(proj, x, s0, wa2p, ba, ng, wout, tb, lc):
    b, t, d = x.shape
    heads, dk, dv = s0.shape[1:]
    n = proj.shape[-1]
    tri, masks, nlev = _gla_tables(lc, heads)
    kern = functools.partial(_gla_kernel, lc=lc, nlev=nlev, heads=heads, dk=dk, dv=dv)
    return pl.pallas_call(
        kern,
        grid=(b, t // tb),
        in_specs=[
            pl.BlockSpec((None, tb, n), lambda i, j: (i, j, 0)),
            pl.BlockSpec((None, tb, d), lambda i, j: (i, j, 0)),
            pl.BlockSpec((None, heads, dk, dv), lambda i, j: (i, 0, 0, 0)),
            _const_spec(wa2p.shape),
            _const_spec((1, heads * dk)),
            _const_spec(tri.shape),
            _const_spec(masks.shape),
            _const_spec((1, heads * dv)),
            _const_spec((heads * dv, d)),
        ],
        out_specs=[
            pl.BlockSpec((None, tb, d), lambda i, j: (i, j, 0)),
            pl.BlockSpec((None, heads, dk, dv), lambda i, j: (i, 0, 0, 0)),
        ],
        out_shape=[
            jax.ShapeDtypeStruct((b, t, d), F32),
            jax.ShapeDtypeStruct(s0.shape, F32),
        ],
        scratch_shapes=[pltpu.VMEM((tb, heads * dv), BF16)],
        compiler_params=pltpu.CompilerParams(
            dimension_semantics=("parallel", "arbitrary"), vmem_limit_bytes=VMEM_LIMIT_BYTES),
        name="gla",
    )(proj, x, s0, wa2p, ba, jnp.asarray(tri, BF16), jnp.asarray(masks), ng, wout)


def _ffn_kernel(x_hbm, g_ref, wup_ref, cw_ref, cb_ref, wdn_ref, cs_ref, gf_ref,
                xo_hbm, nc_ref, xbuf, obuf, hbuf, ubuf, abuf, in_sem, out_sem,
                *, cwid, final_norm, seq_slabs):
    bi, t = pl.program_id(0), pl.program_id(1)
    nt = pl.num_programs(1)
    nsteps = pl.num_programs(0) * nt
    step = bi * nt + t
    slot = step % 2
    lslab, nslab = xbuf.shape[1], xbuf.shape[2]
    tm = lslab * nslab
    dff = wdn_ref.shape[0]

    def slab_copies(to_vmem, b_idx, t_idx, sl):
        copies = []
        for s_i in range(nslab):
            ref = x_hbm if to_vmem else xo_hbm
            if seq_slabs:
                hbm = ref.at[s_i]
            else:
                hbm = ref.at[b_idx, pl.ds(t_idx * tm + s_i * lslab, lslab), :]
            if to_vmem:
                copies.append(pltpu.make_async_copy(hbm, xbuf.at[sl, :, s_i, :], in_sem.at[sl, s_i]))
            else:
                copies.append(pltpu.make_async_copy(obuf.at[sl, :, s_i, :], hbm, out_sem.at[sl, s_i]))
        return copies

    @pl.when(step == 0)
    def _():
        for cp in slab_copies(True, bi, t, slot):
            cp.start()

    @pl.when(step + 1 < nsteps)
    def _():
        nxt = step + 1
        for cp in slab_copies(True, nxt // nt, nxt % nt, 1 - slot):
            cp.start()

    if not seq_slabs:
        @pl.when(t == 0)
        def _():
            nc_ref[...] = cs_ref[...]

    for cp in slab_copies(True, bi, t, slot):
        cp.wait()

    @pl.when(step >= 2)
    def _():
        for cp in slab_copies(False, bi, t, slot):
            cp.wait()

    d = xbuf.shape[-1]
    hbuf[...] = _rms(xbuf[slot].reshape(tm, d), g_ref[...]).astype(BF16)
    hdr = 2 * SUBLANES
    sub = lax.broadcasted_iota(jnp.int32, (SUBLANES, 1), 0)

    def conv(c0, slot):
        cols = slice(c0, c0 + cwid)
        u = _dot(hbuf[...], wup_ref[:, cols])
        ub = ubuf.at[slot]
        if seq_slabs:
            for r in range(CONV_W - 1):
                ub[r * SUBLANES:(r + 1) * SUBLANES, :] = jnp.concatenate(
                    [cs_ref[s_i, r:r + 1, cols] for s_i in range(nslab)], axis=0)
                for s_i in range(nslab):
                    row = tm - (CONV_W - 1 - r) * SUBLANES + s_i
                    nc_ref[s_i, r:r + 1, cols] = u[row:row + 1, :]
        else:
            ub[0:SUBLANES, :] = jnp.where(sub == 0, nc_ref[0:1, cols],
                                          pltpu.roll(u[tm - 2 * SUBLANES:tm - SUBLANES, :], 1, 0))
            ub[SUBLANES:hdr, :] = jnp.where(sub == 0, nc_ref[1:2, cols],
                                            pltpu.roll(u[tm - SUBLANES:tm, :], 1, 0))
            nc_ref[0:1, cols] = u[tm - SUBLANES - 1:tm - SUBLANES, :]
            nc_ref[1:2, cols] = u[tm - 1:tm, :]
        ub[hdr:hdr + tm, :] = u
        return (cb_ref[:, cols] + ub[0:tm, :] * cw_ref[0:1, cols]
                + ub[SUBLANES:SUBLANES + tm, :] * cw_ref[1:2, cols] + u * cw_ref[2:3, cols])

    for c in range(dff // cwid):
        gate = conv(c * cwid, 2 * (c % 2))
        val = conv(dff + c * cwid, 2 * (c % 2) + 1)
        abuf[:, c * cwid:(c + 1) * cwid] = (_silu(gate) * val).astype(BF16)
    out = xbuf[slot].reshape(tm, d) + _dot(abuf[...], wdn_ref[...])
    if final_norm:
        out = _rms(out, gf_ref[...])
    obuf[slot] = out.reshape(lslab, nslab, d)
    for cp in slab_copies(False, bi, t, slot):
        cp.start()

    @pl.when(step == nsteps - 1)
    def _():
        for cp in slab_copies(False, bi, t, slot):
            cp.wait()

        @pl.when(step >= 1)
        def _():
            for cp in slab_copies(False, bi, t, 1 - slot):
                cp.wait()


def _conv_ffn(x, g, wup, cw, cb, wdn, cs, gf, layer, tm, final_norm, cwid=256):
    b, t, d = x.shape
    dff = wdn.shape[1]
    seq_slabs = b == SUBLANES and t < FFN_TILE
    if seq_slabs:
        tm = b * t
        grid = (1, 1)
        cs_spec = pl.BlockSpec((None, b, CONV_W - 1, 2 * dff), lambda i, j: (layer, 0, 0, 0))
        nc_spec = pl.BlockSpec((b, CONV_W - 1, 2 * dff), lambda i, j: (0, 0, 0))
    else:
        grid = (b, t // tm)
        cs_spec = pl.BlockSpec((None, None, CONV_W - 1, 2 * dff), lambda i, j: (layer, i, 0, 0))
        nc_spec = pl.BlockSpec((None, CONV_W - 1, 2 * dff), lambda i, j: (i, 0, 0))
    kern = functools.partial(_ffn_kernel, cwid=cwid, final_norm=final_norm, seq_slabs=seq_slabs)
    return pl.pallas_call(
        kern,
        grid=grid,
        in_specs=[
            pl.BlockSpec(memory_space=pl.ANY),
            _layer_spec((1, d), layer),
            _layer_spec((d, 2 * dff), layer),
            _layer_spec((CONV_W, 2 * dff), layer),
            _layer_spec((1, 2 * dff), layer),
            _layer_spec((dff, d), layer),
            cs_spec,
            _const_spec((1, d)),
        ],
        out_specs=[pl.BlockSpec(memory_space=pl.ANY), nc_spec],
        out_shape=[
            jax.ShapeDtypeStruct((b, t, d), F32),
            jax.ShapeDtypeStruct((b, CONV_W - 1, 2 * dff), F32),
        ],
        scratch_shapes=[
            pltpu.VMEM((2, tm // SUBLANES, SUBLANES, d), F32),
            pltpu.VMEM((2, tm // SUBLANES, SUBLANES, d), F32),
            pltpu.VMEM((tm, d), BF16),
            pltpu.VMEM((4, tm + 2 * SUBLANES, cwid), F32),
            pltpu.VMEM((tm, dff), BF16),
            pltpu.SemaphoreType.DMA((2, SUBLANES)),
            pltpu.SemaphoreType.DMA((2, SUBLANES)),
        ],
        compiler_params=pltpu.CompilerParams(
            dimension_semantics=("arbitrary", "arbitrary"), vmem_limit_bytes=VMEM_LIMIT_BYTES),
        name="conv_ffn",
    )(x, g, wup, cw, cb, wdn, cs, gf)


def _tile(t, pref):
    return pref if t % pref == 0 else t


def _trunk(x, pos0, ret_state, gla_state, conv_state, w):
    b, t, d = x.shape
    tm = _tile(b * t, PROJ_TILE)
    tb = _tile(t, RET_TILE)
    ret_lc = _tile(tb, RET_CHUNK)
    gla_tb = _tile(t, GLA_TILE)
    gla_lc = _tile(gla_tb, GLA_CHUNK)

    half = ret_state.shape[-2] // 2
    inv = ROPE_BASE ** (-jnp.arange(half, dtype=F32) / half)
    blk = min(t, 64)
    ang_a = (pos0 + blk * jnp.arange(t // blk, dtype=jnp.int32)).astype(F32)[:, None] * inv[None, :]
    ang_r = jnp.arange(blk, dtype=F32)[:, None] * inv[None, :]
    ca, sa = jnp.cos(ang_a)[:, None, :], jnp.sin(ang_a)[:, None, :]
    cr, sr = jnp.cos(ang_r)[None, :, :], jnp.sin(ang_r)[None, :, :]
    cos = (ca * cr - sa * sr).reshape(t, half)
    sin = (sa * cr + ca * sr).reshape(t, half)

    proj = _norm_proj(x.reshape(b * t, d), w["norm_mix"][0:1], w["ret_w_in"], tm)
    x, ret_s = _retention(proj.reshape(b, t, -1), x, cos, sin, ret_state, w["ret_gn_g"],
                          w["ret_w_out"], tb, ret_lc)
    ffn = (w["norm_ffn"], w["ffn_w_up"], w["ffn_conv_w"], w["ffn_conv_b"], w["ffn_w_down"],
           conv_state, w["norm_final"])
    x, conv0 = _conv_ffn(x, *ffn, 0, _tile(t, FFN_TILE), False)
    proj = _norm_proj(x.reshape(b * t, d), w["norm_mix"][1:2], w["gla_w_in"], tm)
    x, gla_s = _gla(proj.reshape(b, t, -1), x, gla_state, w["gla_w_a2"], w["gla_b_a"],
                    w["gla_norm_g"], w["gla_w_out"], gla_tb, gla_lc)
    x, conv1 = _conv_ffn(x, *ffn, 1, _tile(t, FFN_TILE), True)
    return x, ret_s[None], gla_s[None], jnp.stack([conv0, conv1])


def kernel(x_prompt, x_sample, state_ret, state_gla, cache_conv, norm_mix, norm_ffn, norm_final,
           ret_w_in, ret_gn_g, ret_w_out, gla_w_in, gla_w_a2, gla_b_a, gla_norm_g, gla_w_out,
           ffn_w_up, ffn_conv_w, ffn_conv_b, ffn_w_down):
    assert norm_mix.shape[0] == 2 and ret_w_in.shape[0] == 1 and gla_w_in.shape[0] == 1
    d = x_prompt.shape[-1]
    gla_qk = gla_w_a2.shape[-1]
    gla_v = gla_w_out.shape[1]
    n_main = 2 * gla_qk + 2 * gla_v
    gla_w_in_p = jnp.concatenate(
        [gla_w_in[0, :, :n_main],
         jnp.pad(gla_w_in[0, :, n_main:], ((0, 0), (0, LANES - GLA_GATE_RANK)))], axis=1)
    w = dict(
        norm_mix=norm_mix, norm_ffn=norm_ffn[:, None, :], norm_final=norm_final.reshape(1, d),
        ret_w_in=ret_w_in[0].astype(BF16),
        ret_gn_g=ret_gn_g[0].reshape(1, -1),
        ret_w_out=ret_w_out[0].astype(BF16),
        gla_w_in=gla_w_in_p.astype(BF16),
        gla_w_a2=jnp.pad(gla_w_a2[0], ((0, LANES - GLA_GATE_RANK), (0, 0))).astype(BF16),
        gla_b_a=gla_b_a[0].reshape(1, -1),
        gla_norm_g=gla_norm_g[0].reshape(1, -1),
        gla_w_out=gla_w_out[0].astype(BF16),
        ffn_w_up=ffn_w_up.astype(BF16), ffn_conv_w=ffn_conv_w, ffn_conv_b=ffn_conv_b[:, None, :],
        ffn_w_down=ffn_w_down.astype(BF16),
    )
    bp = x_prompt.shape[0]
    ret0 = jnp.zeros((bp,) + state_ret.shape[2:], F32)
    gla0 = jnp.zeros((bp,) + state_gla.shape[2:], F32)
    conv0 = jnp.zeros((cache_conv.shape[0], bp) + cache_conv.shape[2:], F32)
    y_p, ret_p, gla_p, conv_p = _trunk(x_prompt, 0, ret0, gla0, conv0, w)
    y_s, ret_s, gla_s, conv_s = _trunk(x_sample, PAST_LEN, state_ret[0], state_gla[0],
                                       cache_conv, w)
    return (y_p, y_s, ret_p, ret_s, gla_p, gla_s, conv_p, conv_s)
```

```python
import functools
import math

import numpy as np
import jax
import jax.numpy as jnp
from jax import lax
from jax.experimental import pallas as pl
from jax.experimental.pallas import tpu as pltpu

EPS = 1e-6
ROPE_BASE = 10000.0
GLA_TAU = 16.0
GLA_GATE_RANK = 16
CONV_W = 3
PAST_LEN = 2048
GLA_CHUNK = 64
RET_CHUNK = 256
PROJ_TILE = 1024
RET_TILE = 512
GLA_TILE = 512
FFN_TILE = 512
LANES = 128
SUBLANES = 8
VMEM_LIMIT_BYTES = 56 * 1024 * 1024

F32 = jnp.float32
BF16 = jnp.bfloat16


def _const_spec(shape):
    nd = len(shape)
    return pl.BlockSpec(shape, lambda *_: (0,) * nd, pipeline_mode=pl.Buffered(1))


def _layer_spec(shape, layer):
    nd = len(shape)
    return pl.BlockSpec((None,) + tuple(shape), lambda *_: (layer,) + (0,) * nd,
                        pipeline_mode=pl.Buffered(1))


def _rms(x, g):
    ms = jnp.mean(x * x, axis=-1, keepdims=True)
    return x * lax.rsqrt(ms + EPS) * g


def _silu(x):
    return x * (1.0 / (1.0 + jnp.exp(-x)))


def _dot(a, b):
    return jnp.dot(a, b, preferred_element_type=F32)


def _dot_nt(a, b):
    return lax.dot_general(a, b, (((1,), (1,)), ((), ())), preferred_element_type=F32)


def _dot_tn(a, b):
    return lax.dot_general(a, b, (((0,), (0,)), ((), ())), preferred_element_type=F32)


def _norm_proj_kernel(x_ref, g_ref, w_ref, o_ref, *, tn):
    h = _rms(x_ref[...], g_ref[...]).astype(BF16)
    n = w_ref.shape[1]
    for n0 in range(0, n, tn):
        n1 = min(n0 + tn, n)
        o_ref[:, n0:n1] = _dot(h, w_ref[:, n0:n1]).astype(o_ref.dtype)


def _norm_proj(x2, g, w, tm, tn=512):
    m, d = x2.shape
    n = w.shape[1]
    return pl.pallas_call(
        functools.partial(_norm_proj_kernel, tn=tn),
        grid=(m // tm,),
        in_specs=[
            pl.BlockSpec((tm, d), lambda i: (i, 0)),
            _const_spec((1, d)),
            _const_spec((d, n)),
        ],
        out_specs=pl.BlockSpec((tm, n), lambda i: (i, 0)),
        out_shape=jax.ShapeDtypeStruct((m, n), BF16),
        compiler_params=pltpu.CompilerParams(
            dimension_semantics=("parallel",), vmem_limit_bytes=VMEM_LIMIT_BYTES),
        name="norm_proj",
    )(x2, g, w)


def _ret_kernel(proj_ref, x_ref, cos_ref, sin_ref, s0_ref, dmat_ref, gn_ref, wout_ref,
                xo_ref, s_ref, y_scr, *, lc, heads, dk, dv, log_gamma):
    t = pl.program_id(1)

    @pl.when(t == 0)
    def _():
        s_ref[...] = s0_ref[...]

    tb = proj_ref.shape[0]
    half = dk // 2
    k_off = heads * dk
    v_off = 2 * heads * dk
    g_off = v_off + heads * dv
    scale = dk ** -0.5
    idx = lax.broadcasted_iota(jnp.int32, (lc, 1), 0).astype(F32)

    def rot(u, cos, sin):
        u1, u2 = u[:, :half], u[:, half:]
        return jnp.concatenate([u1 * cos - u2 * sin, u1 * sin + u2 * cos], axis=-1)

    for c in range(tb // lc):
        r0 = c * lc
        cos = cos_ref[r0:r0 + lc, :]
        sin = sin_ref[r0:r0 + lc, :]
        for h in range(heads):
            lg = log_gamma[h]
            q = rot(proj_ref[r0:r0 + lc, h * dk:(h + 1) * dk].astype(F32), cos, sin)
            k = rot(proj_ref[r0:r0 + lc, k_off + h * dk:k_off + (h + 1) * dk].astype(F32),
                    cos, sin) * scale
            v = proj_ref[r0:r0 + lc, v_off + h * dv:v_off + (h + 1) * dv]
            g = proj_ref[r0:r0 + lc, g_off + h * dv:g_off + (h + 1) * dv].astype(F32)
            s_old = s_ref[h]
            scores = _dot_nt(q.astype(BF16), k.astype(BF16)) * dmat_ref[h]
            o = _dot(scores.astype(BF16), v)
            q_dec = jnp.exp(lg * (idx + 1.0))
            o = o + _dot((q * q_dec).astype(BF16), s_old.astype(BF16))
            k_dec = jnp.exp(lg * (lc - 1.0 - idx))
            s_ref[h] = math.exp(lg * lc) * s_old + _dot_tn((k * k_dec).astype(BF16), v)
            mu = jnp.mean(o, axis=-1, keepdims=True)
            d = o - mu
            var = jnp.mean(d * d, axis=-1, keepdims=True)
            on = d * lax.rsqrt(var + EPS) * gn_ref[:, h * dv:(h + 1) * dv]
            y_scr[r0:r0 + lc, h * dv:(h + 1) * dv] = (_silu(g) * on).astype(BF16)

    xo_ref[...] = x_ref[...] + _dot(y_scr[...], wout_ref[...])


def _retention(proj, x, cos, sin, s0, gn, wout, tb, lc):
    b, t, d = x.shape
    heads, dk, dv = s0.shape[1:]
    n = proj.shape[-1]
    log_gamma = [math.log1p(-(2.0 ** (-5.0 - h))) for h in range(heads)]
    ii = np.arange(lc, dtype=np.float64)
    diff = ii[:, None] - ii[None, :]
    dmat = np.stack([np.where(diff >= 0, np.exp(lg * np.maximum(diff, 0.0)), 0.0)
                     for lg in log_gamma]).astype(np.float32)
    kern = functools.partial(_ret_kernel, lc=lc, heads=heads, dk=dk, dv=dv,
                             log_gamma=tuple(log_gamma))
    return pl.pallas_call(
        kern,
        grid=(b, t // tb),
        in_specs=[
            pl.BlockSpec((None, tb, n), lambda i, j: (i, j, 0)),
            pl.BlockSpec((None, tb, d), lambda i, j: (i, j, 0)),
            pl.BlockSpec((tb, dk // 2), lambda i, j: (j, 0)),
            pl.BlockSpec((tb, dk // 2), lambda i, j: (j, 0)),
            pl.BlockSpec((None, heads, dk, dv), lambda i, j: (i, 0, 0, 0)),
            _const_spec((heads, lc, lc)),
            _const_spec((1, heads * dv)),
            _const_spec((heads * dv, d)),
        ],
        out_specs=[
            pl.BlockSpec((None, tb, d), lambda i, j: (i, j, 0)),
            pl.BlockSpec((None, heads, dk, dv), lambda i, j: (i, 0, 0, 0)),
        ],
        out_shape=[
            jax.ShapeDtypeStruct((b, t, d), F32),
            jax.ShapeDtypeStruct(s0.shape, F32),
        ],
        scratch_shapes=[pltpu.VMEM((tb, heads * dv), BF16)],
        compiler_params=pltpu.CompilerParams(
            dimension_semantics=("parallel", "arbitrary"), vmem_limit_bytes=VMEM_LIMIT_BYTES),
        name="retention",
    )(proj, x, cos, sin, s0, jnp.asarray(dmat), gn, wout)


def _gla_tables(lc, heads):
    nlev = int(math.log2(lc))
    i = np.arange(lc)[:, None]
    j = np.arange(lc)[None, :]
    masks = []
    for lev in range(nlev):
        h = lc >> (lev + 1)
        same = (i // (2 * h)) == (j // (2 * h))
        masks.append(same & ((i // h) % 2 == 1) & ((j // h) % 2 == 0))
    masks.append(i == j)
    masks = np.tile(np.stack(masks).astype(np.float32), (1, 1, heads))
    tri = (j <= i).astype(np.float32)
    return tri, masks, nlev


def _gla_kernel(proj_ref, x_ref, s0_ref, wa2_ref, ba_ref, tri_ref, masks_ref, ng_ref, wout_ref,
                xo_ref, s_ref, y_scr, *, lc, nlev, heads, dk, dv):
    t = pl.program_id(1)

    @pl.when(t == 0)
    def _():
        s_ref[...] = s0_ref[...]

    tb = proj_ref.shape[0]
    qk = heads * dk
    v_off = 2 * qk
    r_off = v_off + heads * dv
    a_off = r_off + heads * dv
    scale = dk ** -0.5
    row = lax.broadcasted_iota(jnp.int32, (tb, 1), 0)
    zero_k = jnp.zeros((lc, dk), BF16)

    def blockdiag(kk):
        return jnp.concatenate(
            [jnp.concatenate([kk[:, h * dk:(h + 1) * dk] if g == h else zero_k
                              for g in range(heads)], axis=1) for h in range(heads)], axis=0)

    def block_rows(x, period, r):
        return jnp.concatenate(
            [jnp.broadcast_to(x[m:m + 1, :], (period, x.shape[1]))
             for m in range(r, x.shape[0], period)], axis=0)

    def level_exponent(h, b, lg):
        if h == 1:
            return jnp.where((row & 1) == 1, lg, 0.0)
        if h == 2:
            r4 = row & 3
            lg_next = pltpu.roll(lg, tb - 1, 0)
            lg_prev = pltpu.roll(lg, 1, 0)
            return jnp.where(r4 == 0, lg_next,
                             jnp.where(r4 == 1, 0.0, jnp.where(r4 == 2, lg, lg_prev + lg)))
        return -jnp.abs(b - block_rows(b, 2 * h, h - 1))

    nc = tb // lc
    chunks = [slice(c * lc, (c + 1) * lc) for c in range(nc)]
    q = proj_ref[:, 0:qk].astype(F32)
    k = proj_ref[:, qk:2 * qk].astype(F32) * scale
    z = _dot(proj_ref[:, a_off:a_off + LANES], wa2_ref[...]) + ba_ref[...]
    lg = (jnp.minimum(z, 0.0) - jnp.log(1.0 + jnp.exp(-jnp.abs(z)))) * (1.0 / GLA_TAU)
    lg_hi = lg.astype(BF16)
    lg_lo = (lg - lg_hi.astype(F32)).astype(BF16)
    tri = tri_ref[...]
    b = jnp.concatenate([_dot(tri, lg_hi[rs]) + _dot(tri, lg_lo[rs]) for rs in chunks], axis=0)
    b_last = block_rows(b, lc, lc - 1)
    qx = (q * jnp.exp(b)).astype(BF16)
    kx = (k * jnp.exp(b_last - b)).astype(BF16)
    f_last = jnp.exp(b_last)

    qb, kb = q.astype(BF16), k.astype(BF16)
    p = [masks_ref[nlev] * _dot_nt(qb[rs], blockdiag(kb[rs])) for rs in chunks]
    for lev in range(nlev):
        h = lc >> (lev + 1)
        second = ((row // h) & 1) == 1
        m = (jnp.where(second, q, k) * jnp.exp(level_exponent(h, b, lg))).astype(BF16)
        for c, rs in enumerate(chunks):
            p[c] = p[c] + masks_ref[lev] * _dot_nt(m[rs], blockdiag(m[rs]))

    v = [[proj_ref[rs, v_off + h * dv:v_off + (h + 1) * dv] for h in range(heads)]
         for rs in chunks]
    ds = [[_dot_tn(kx[rs, h * dk:(h + 1) * dk], v[c][h]) for h in range(heads)]
          for c, rs in enumerate(chunks)]

    s = [s_ref[h] for h in range(heads)]
    for c, rs in enumerate(chunks):
        p_bf = p[c].astype(BF16)
        for h in range(heads):
            ks = slice(h * dk, (h + 1) * dk)
            g0 = (h * lc) // LANES * LANES
            off = h * lc - g0
            lhs = jnp.concatenate([qx[rs, ks], p_bf[:, g0:g0 + LANES]], axis=1)
            rhs = [s[h].astype(BF16)]
            if off:
                rhs.append(jnp.zeros((off, dv), BF16))
            rhs.append(v[c][h])
            if LANES - off - lc:
                rhs.append(jnp.zeros((LANES - off - lc, dv), BF16))
            o = _dot(lhs, jnp.concatenate(rhs, axis=0))
            fl = f_last[c * lc:c * lc + 1, ks]
            decay = jnp.broadcast_to(fl, (dk, dk)).T
            s[h] = s[h] * jnp.concatenate([decay] * (dv // dk), axis=1) + ds[c][h]
            r = proj_ref[rs, r_off + h * dv:r_off + (h + 1) * dv].astype(F32)
            ms = jnp.mean(o * o, axis=-1, keepdims=True)
            on = o * lax.rsqrt(ms + EPS) * ng_ref[:, h * dv:(h + 1) * dv]
            y_scr[rs, h * dv:(h + 1) * dv] = (_silu(r) * on).astype(BF16)
    for h in range(heads):
        s_ref[h] = s[h]

    xo_ref[...] = x_ref[...] + _dot(y_scr[...], wout_ref[...])


def _gla(proj, x, s0, wa2p, ba, ng, wout, tb, lc):
    b, t, d = x.shape
    heads, dk, dv = s0.shape[1:]
    n = proj.shape[-1]
    tri, masks, nlev = _gla_tables(lc, heads)
    kern = functools.partial(_gla_kernel, lc=lc, nlev=nlev, heads=heads, dk=dk, dv=dv)
    return pl.pallas_call(
        kern,
        grid=(b, t // tb),
        in_specs=[
            pl.BlockSpec((None, tb, n), lambda i, j: (i, j, 0)),
            pl.BlockSpec((None, tb, d), lambda i, j: (i, j, 0)),
            pl.BlockSpec((None, heads, dk, dv), lambda i, j: (i, 0, 0, 0)),
            _const_spec(wa2p.shape),
            _const_spec((1, heads * dk)),
            _const_spec(tri.shape),
            _const_spec(masks.shape),
            _const_spec((1, heads * dv)),
            _const_spec((heads * dv, d)),
        ],
        out_specs=[
            pl.BlockSpec((None, tb, d), lambda i, j: (i, j, 0)),
            pl.BlockSpec((None, heads, dk, dv), lambda i, j: (i, 0, 0, 0)),
        ],
        out_shape=[
            jax.ShapeDtypeStruct((b, t, d), F32),
            jax.ShapeDtypeStruct(s0.shape, F32),
        ],
        scratch_shapes=[pltpu.VMEM((tb, heads * dv), BF16)],
        compiler_params=pltpu.CompilerParams(
            dimension_semantics=("parallel", "arbitrary"), vmem_limit_bytes=VMEM_LIMIT_BYTES),
        name="gla",
    )(proj, x, s0, wa2p, ba, jnp.asarray(tri, BF16), jnp.asarray(masks), ng, wout)


def _ffn_kernel(x_hbm, g_ref, wup_ref, cw_ref, cb_ref, wdn_ref, cs_ref, gf_ref,
                xo_hbm, nc_ref, xbuf, obuf, hbuf, ubuf, abuf, in_sem, out_sem,
                *, cwid, final_norm, seq_slabs):
    bi, t = pl.program_id(0), pl.program_id(1)
    nt = pl.num_programs(1)
    nsteps = pl.num_programs(0) * nt
    step = bi * nt + t
    slot = step % 2
    lslab, nslab = xbuf.shape[1], xbuf.shape[2]
    tm = lslab * nslab
    dff = wdn_ref.shape[0]

    def slab_copies(to_vmem, b_idx, t_idx, sl):
        copies = []
        for s_i in range(nslab):
            ref = x_hbm if to_vmem else xo_hbm
            if seq_slabs:
                hbm = ref.at[s_i]
            else:
                hbm = ref.at[b_idx, pl.ds(t_idx * tm + s_i * lslab, lslab), :]
            if to_vmem:
                copies.append(pltpu.make_async_copy(hbm, xbuf.at[sl, :, s_i, :], in_sem.at[sl, s_i]))
            else:
                copies.append(pltpu.make_async_copy(obuf.at[sl, :, s_i, :], hbm, out_sem.at[sl, s_i]))
        return copies

    @pl.when(step == 0)
    def _():
        for cp in slab_copies(True, bi, t, slot):
            cp.start()

    @pl.when(step + 1 < nsteps)
    def _():
        nxt = step + 1
        for cp in slab_copies(True, nxt // nt, nxt % nt, 1 - slot):
            cp.start()

    if not seq_slabs:
        @pl.when(t == 0)
        def _():
            nc_ref[...] = cs_ref[...]

    for cp in slab_copies(True, bi, t, slot):
        cp.wait()

    @pl.when(step >= 2)
    def _():
        for cp in slab_copies(False, bi, t, slot):
            cp.wait()

    d = xbuf.shape[-1]
    hbuf[...] = _rms(xbuf[slot].reshape(tm, d), g_ref[...]).astype(BF16)
    hdr = 2 * SUBLANES
    sub = lax.broadcasted_iota(jnp.int32, (SUBLANES, 1), 0)

    def conv(c0, slot):
        cols = slice(c0, c0 + cwid)
        u = _dot(hbuf[...], wup_ref[:, cols])
        ub = ubuf.at[slot]
        if seq_slabs:
            for r in range(CONV_W - 1):
                ub[r * SUBLANES:(r + 1) * SUBLANES, :] = jnp.concatenate(
                    [cs_ref[s_i, r:r + 1, cols] for s_i in range(nslab)], axis=0)
                for s_i in range(nslab):
                    row = tm - (CONV_W - 1 - r) * SUBLANES + s_i
                    nc_ref[s_i, r:r + 1, cols] = u[row:row + 1, :]
        else:
            ub[0:SUBLANES, :] = jnp.where(sub == 0, nc_ref[0:1, cols],
                                          pltpu.roll(u[tm - 2 * SUBLANES:tm - SUBLANES, :], 1, 0))
            ub[SUBLANES:hdr, :] = jnp.where(sub == 0, nc_ref[1:2, cols],
                                            pltpu.roll(u[tm - SUBLANES:tm, :], 1, 0))
            nc_ref[0:1, cols] = u[tm - SUBLANES - 1:tm - SUBLANES, :]
            nc_ref[1:2, cols] = u[tm - 1:tm, :]
        ub[hdr:hdr + tm, :] = u
        return (cb_ref[:, cols] + ub[0:tm, :] * cw_ref[0:1, cols]
                + ub[SUBLANES:SUBLANES + tm, :] * cw_ref[1:2, cols] + u * cw_ref[2:3, cols])

    for c in range(dff // cwid):
        gate = conv(c * cwid, 2 * (c % 2))
        val = conv(dff + c * cwid, 2 * (c % 2) + 1)
        abuf[:, c * cwid:(c + 1) * cwid] = (_silu(gate) * val).astype(BF16)
    out = xbuf[slot].reshape(tm, d) + _dot(abuf[...], wdn_ref[...])
    if final_norm:
        out = _rms(out, gf_ref[...])
    obuf[slot] = out.reshape(lslab, nslab, d)
    for cp in slab_copies(False, bi, t, slot):
        cp.start()

    @pl.when(step == nsteps - 1)
    def _():
        for cp in slab_copies(False, bi, t, slot):
            cp.wait()

        @pl.when(step >= 1)
        def _():
            for cp in slab_copies(False, bi, t, 1 - slot):
                cp.wait()


def _conv_ffn(x, g, wup, cw, cb, wdn, cs, gf, layer, tm, final_norm, cwid=256):
    b, t, d = x.shape
    dff = wdn.shape[1]
    seq_slabs = b == SUBLANES and t < FFN_TILE
    if seq_slabs:
        tm = b * t
        grid = (1, 1)
        cs_spec = pl.BlockSpec((None, b, CONV_W - 1, 2 * dff), lambda i, j: (layer, 0, 0, 0))
        nc_spec = pl.BlockSpec((b, CONV_W - 1, 2 * dff), lambda i, j: (0, 0, 0))
    else:
        grid = (b, t // tm)
        cs_spec = pl.BlockSpec((None, None, CONV_W - 1, 2 * dff), lambda i, j: (layer, i, 0, 0))
        nc_spec = pl.BlockSpec((None, CONV_W - 1, 2 * dff), lambda i, j: (i, 0, 0))
    kern = functools.partial(_ffn_kernel, cwid=cwid, final_norm=final_norm, seq_slabs=seq_slabs)
    return pl.pallas_call(
        kern,
        grid=grid,
        in_specs=[
            pl.BlockSpec(memory_space=pl.ANY),
            _layer_spec((1, d), layer),
            _layer_spec((d, 2 * dff), layer),
            _layer_spec((CONV_W, 2 * dff), layer),
            _layer_spec((1, 2 * dff), layer),
            _layer_spec((dff, d), layer),
            cs_spec,
            _const_spec((1, d)),
        ],
        out_specs=[pl.BlockSpec(memory_space=pl.ANY), nc_spec],
        out_shape=[
            jax.ShapeDtypeStruct((b, t, d), F32),
            jax.ShapeDtypeStruct((b, CONV_W - 1, 2 * dff), F32),
        ],
        scratch_shapes=[
            pltpu.VMEM((2, tm // SUBLANES, SUBLANES, d), F32),
            pltpu.VMEM((2, tm // SUBLANES, SUBLANES, d), F32),
            pltpu.VMEM((tm, d), BF16),
            pltpu.VMEM((4, tm + 2 * SUBLANES, cwid), F32),
            pltpu.VMEM((tm, dff), BF16),
            pltpu.SemaphoreType.DMA((2, SUBLANES)),
            pltpu.SemaphoreType.DMA((2, SUBLANES)),
        ],
        compiler_params=pltpu.CompilerParams(
            dimension_semantics=("arbitrary", "arbitrary"), vmem_limit_bytes=VMEM_LIMIT_BYTES),
        name="conv_ffn",
    )(x, g, wup, cw, cb, wdn, cs, gf)


def _tile(t, pref):
    return pref if t % pref == 0 else t


def _trunk(x, pos0, ret_state, gla_state, conv_state, w):
    b, t, d = x.shape
    tm = _tile(b * t, PROJ_TILE)
    tb = _tile(t, RET_TILE)
    ret_lc = _tile(tb, RET_CHUNK)
    gla_tb = _tile(t, GLA_TILE)
    gla_lc = _tile(gla_tb, GLA_CHUNK)

    half = ret_state.shape[-2] // 2
    inv = ROPE_BASE ** (-jnp.arange(half, dtype=F32) / half)
    blk = min(t, 64)
    ang_a = (pos0 + blk * jnp.arange(t // blk, dtype=jnp.int32)).astype(F32)[:, None] * inv[None, :]
    ang_r = jnp.arange(blk, dtype=F32)[:, None] * inv[None, :]
    ca, sa = jnp.cos(ang_a)[:, None, :], jnp.sin(ang_a)[:, None, :]
    cr, sr = jnp.cos(ang_r)[None, :, :], jnp.sin(ang_r)[None, :, :]
    cos = (ca * cr - sa * sr).reshape(t, half)
    sin = (sa * cr + ca * sr).reshape(t, half)

    proj = _norm_proj(x.reshape(b * t, d), w["norm_mix"][0:1], w["ret_w_in"], tm)
    x, ret_s = _retention(proj.reshape(b, t, -1), x, cos, sin, ret_state, w["ret_gn_g"],
                          w["ret_w_out"], tb, ret_lc)
    ffn = (w["norm_ffn"], w["ffn_w_up"], w["ffn_conv_w"], w["ffn_conv_b"], w["ffn_w_down"],
           conv_state, w["norm_final"])
    x, conv0 = _conv_ffn(x, *ffn, 0, _tile(t, FFN_TILE), False)
    proj = _norm_proj(x.reshape(b * t, d), w["norm_mix"][1:2], w["gla_w_in"], tm)
    x, gla_s = _gla(proj.reshape(b, t, -1), x, gla_state, w["gla_w_a2"], w["gla_b_a"],
                    w["gla_norm_g"], w["gla_w_out"], gla_tb, gla_lc)
    x, conv1 = _conv_ffn(x, *ffn, 1, _tile(t, FFN_TILE), True)
    return x, ret_s[None], gla_s[None], jnp.stack([conv0, conv1])


def kernel(x_prompt, x_sample, state_ret, state_gla, cache_conv, norm_mix, norm_ffn, norm_final,
           ret_w_in, ret_gn_g, ret_w_out, gla_w_in, gla_w_a2, gla_b_a, gla_norm_g, gla_w_out,
           ffn_w_up, ffn_conv_w, ffn_conv_b, ffn_w_down):
    assert norm_mix.shape[0] == 2 and ret_w_in.shape[0] == 1 and gla_w_in.shape[0] == 1
    d = x_prompt.shape[-1]
    gla_qk = gla_w_a2.shape[-1]
    gla_v = gla_w_out.shape[1]
    n_main = 2 * gla_qk + 2 * gla_v
    gla_w_in_p = jnp.concatenate(
        [gla_w_in[0, :, :n_main],
         jnp.pad(gla_w_in[0, :, n_main:], ((0, 0), (0, LANES - GLA_GATE_RANK)))], axis=1)
    w = dict(
        norm_mix=norm_mix, norm_ffn=norm_ffn[:, None, :], norm_final=norm_final.reshape(1, d),
        ret_w_in=ret_w_in[0].astype(BF16),
        ret_gn_g=ret_gn_g[0].reshape(1, -1),
        ret_w_out=ret_w_out[0].astype(BF16),
        gla_w_in=gla_w_in_p.astype(BF16),
        gla_w_a2=jnp.pad(gla_w_a2[0], ((0, LANES - GLA_GATE_RANK), (0, 0))).astype(BF16),
        gla_b_a=gla_b_a[0].reshape(1, -1),
        gla_norm_g=gla_norm_g[0].reshape(1, -1),
        gla_w_out=gla_w_out[0].astype(BF16),
        ffn_w_up=ffn_w_up.astype(BF16), ffn_conv_w=ffn_conv_w, ffn_conv_b=ffn_conv_b[:, None, :],
        ffn_w_down=ffn_w_down.astype(BF16),
    )
    bp = x_prompt.shape[0]
    ret0 = jnp.zeros((bp,) + state_ret.shape[2:], F32)
    gla0 = jnp.zeros((bp,) + state_gla.shape[2:], F32)
    conv0 = jnp.zeros((cache_conv.shape[0], bp) + cache_conv.shape[2:], F32)
    y_p, ret_p, gla_p, conv_p = _trunk(x_prompt, 0, ret0, gla0, conv0, w)
    y_s, ret_s, gla_s, conv_s = _trunk(x_sample, PAST_LEN, state_ret[0], state_gla[0],
                                       cache_conv, w)
    return (y_p, y_s, ret_p, ret_s, gla_p, gla_s, conv_p, conv_s)
```

```python
import functools
import math

import numpy as np
import jax
import jax.numpy as jnp
from jax import lax
from jax.experimental import pallas as pl
from jax.experimental.pallas import tpu as pltpu

EPS = 1e-6
ROPE_BASE = 10000.0
GLA_TAU = 16.0
GLA_GATE_RANK = 16
CONV_W = 3
PAST_LEN = 2048
GLA_CHUNK = 64
RET_CHUNK = 256
PROJ_TILE = 512
RET_TILE = 512
GLA_TILE = 1024
FFN_TILE = 512
LANES = 128
SUBLANES = 8
VMEM_LIMIT_BYTES = 56 * 1024 * 1024

F32 = jnp.float32
BF16 = jnp.bfloat16


def _const_spec(shape):
    nd = len(shape)
    return pl.BlockSpec(shape, lambda *_: (0,) * nd, pipeline_mode=pl.Buffered(1))


def _layer_spec(shape, layer):
    nd = len(shape)
    return pl.BlockSpec((None,) + tuple(shape), lambda *_: (layer,) + (0,) * nd,
                        pipeline_mode=pl.Buffered(1))


def _rms(x, g):
    ms = jnp.mean(x * x, axis=-1, keepdims=True)
    return x * lax.rsqrt(ms + EPS) * g


def _silu(x):
    return x * (1.0 / (1.0 + jnp.exp(-x)))


def _dot(a, b):
    return jnp.dot(a, b, preferred_element_type=F32)


def _dot_nt(a, b):
    return lax.dot_general(a, b, (((1,), (1,)), ((), ())), preferred_element_type=F32)


def _dot_tn(a, b):
    return lax.dot_general(a, b, (((0,), (0,)), ((), ())), preferred_element_type=F32)


def _norm_proj_kernel(x_ref, g_ref, w_ref, o_ref, *, tn):
    h = _rms(x_ref[...], g_ref[...]).astype(BF16)
    n = w_ref.shape[1]
    for n0 in range(0, n, tn):
        n1 = min(n0 + tn, n)
        o_ref[:, n0:n1] = _dot(h, w_ref[:, n0:n1]).astype(o_ref.dtype)


def _norm_proj(x2, g, w, tm, tn=512):
    m, d = x2.shape
    n = w.shape[1]
    return pl.pallas_call(
        functools.partial(_norm_proj_kernel, tn=tn),
        grid=(m // tm,),
        in_specs=[
            pl.BlockSpec((tm, d), lambda i: (i, 0)),
            _const_spec((1, d)),
            _const_spec((d, n)),
        ],
        out_specs=pl.BlockSpec((tm, n), lambda i: (i, 0)),
        out_shape=jax.ShapeDtypeStruct((m, n), BF16),
        compiler_params=pltpu.CompilerParams(
            dimension_semantics=("parallel",), vmem_limit_bytes=VMEM_LIMIT_BYTES),
        name="norm_proj",
    )(x2, g, w)


def _ret_kernel(proj_ref, x_ref, cos_ref, sin_ref, s0_ref, dmat_ref, gn_ref, wout_ref,
                xo_ref, s_ref, y_scr, *, lc, heads, dk, dv, log_gamma):
    t = pl.program_id(1)

    @pl.when(t == 0)
    def _():
        s_ref[...] = s0_ref[...]

    tb = proj_ref.shape[0]
    half = dk // 2
    k_off = heads * dk
    v_off = 2 * heads * dk
    g_off = v_off + heads * dv
    scale = dk ** -0.5
    idx = lax.broadcasted_iota(jnp.int32, (lc, 1), 0).astype(F32)

    def rot(u, cos, sin):
        u1, u2 = u[:, :half], u[:, half:]
        return jnp.concatenate([u1 * cos - u2 * sin, u1 * sin + u2 * cos], axis=-1)

    for c in range(tb // lc):
        r0 = c * lc
        cos = cos_ref[r0:r0 + lc, :]
        sin = sin_ref[r0:r0 + lc, :]
        for h in range(heads):
            lg = log_gamma[h]
            q = rot(proj_ref[r0:r0 + lc, h * dk:(h + 1) * dk].astype(F32), cos, sin)
            k = rot(proj_ref[r0:r0 + lc, k_off + h * dk:k_off + (h + 1) * dk].astype(F32),
                    cos, sin) * scale
            v = proj_ref[r0:r0 + lc, v_off + h * dv:v_off + (h + 1) * dv]
            g = proj_ref[r0:r0 + lc, g_off + h * dv:g_off + (h + 1) * dv].astype(F32)
            s_old = s_ref[h]
            scores = _dot_nt(q.astype(BF16), k.astype(BF16)) * dmat_ref[h]
            o = _dot(scores.astype(BF16), v)
            q_dec = jnp.exp(lg * (idx + 1.0))
            o = o + _dot((q * q_dec).astype(BF16), s_old.astype(BF16))
            k_dec = jnp.exp(lg * (lc - 1.0 - idx))
            s_ref[h] = math.exp(lg * lc) * s_old + _dot_tn((k * k_dec).astype(BF16), v)
            mu = jnp.mean(o, axis=-1, keepdims=True)
            d = o - mu
            var = jnp.mean(d * d, axis=-1, keepdims=True)
            on = d * lax.rsqrt(var + EPS) * gn_ref[:, h * dv:(h + 1) * dv]
            y_scr[r0:r0 + lc, h * dv:(h + 1) * dv] = (_silu(g) * on).astype(BF16)

    xo_ref[...] = x_ref[...] + _dot(y_scr[...], wout_ref[...])


def _retention(proj, x, cos, sin, s0, gn, wout, tb, lc):
    b, t, d = x.shape
    heads, dk, dv = s0.shape[1:]
    n = proj.shape[-1]
    log_gamma = [math.log1p(-(2.0 ** (-5.0 - h))) for h in range(heads)]
    ii = np.arange(lc, dtype=np.float64)
    diff = ii[:, None] - ii[None, :]
    dmat = np.stack([np.where(diff >= 0, np.exp(lg * np.maximum(diff, 0.0)), 0.0)
                     for lg in log_gamma]).astype(np.float32)
    kern = functools.partial(_ret_kernel, lc=lc, heads=heads, dk=dk, dv=dv,
                             log_gamma=tuple(log_gamma))
    return pl.pallas_call(
        kern,
        grid=(b, t // tb),
        in_specs=[
            pl.BlockSpec((None, tb, n), lambda i, j: (i, j, 0)),
            pl.BlockSpec((None, tb, d), lambda i, j: (i, j, 0)),
            pl.BlockSpec((tb, dk // 2), lambda i, j: (j, 0)),
            pl.BlockSpec((tb, dk // 2), lambda i, j: (j, 0)),
            pl.BlockSpec((None, heads, dk, dv), lambda i, j: (i, 0, 0, 0)),
            _const_spec((heads, lc, lc)),
            _const_spec((1, heads * dv)),
            _const_spec((heads * dv, d)),
        ],
        out_specs=[
            pl.BlockSpec((None, tb, d), lambda i, j: (i, j, 0)),
            pl.BlockSpec((None, heads, dk, dv), lambda i, j: (i, 0, 0, 0)),
        ],
        out_shape=[
            jax.ShapeDtypeStruct((b, t, d), F32),
            jax.ShapeDtypeStruct(s0.shape, F32),
        ],
        scratch_shapes=[pltpu.VMEM((tb, heads * dv), BF16)],
        compiler_params=pltpu.CompilerParams(
            dimension_semantics=("parallel", "arbitrary"), vmem_limit_bytes=VMEM_LIMIT_BYTES),
        name="retention",
    )(proj, x, cos, sin, s0, jnp.asarray(dmat), gn, wout)


def _gla_tables(lc, heads):
    nlev = int(math.log2(lc))
    i = np.arange(lc)[:, None]
    j = np.arange(lc)[None, :]
    masks = []
    for lev in range(nlev):
        h = lc >> (lev + 1)
        same = (i // (2 * h)) == (j // (2 * h))
        masks.append(same & ((i // h) % 2 == 1) & ((j // h) % 2 == 0))
    masks.append(i == j)
    masks = np.tile(np.stack(masks).astype(np.float32), (1, 1, heads))
    tri = (j <= i).astype(np.float32)
    return tri, masks, nlev


def _gla_kernel(proj_ref, x_ref, s0_ref, wa2_ref, ba_ref, tri_ref, masks_ref, ng_ref, wout_ref,
                xo_ref, s_ref, y_scr, *, lc, nlev, heads, dk, dv):
    t = pl.program_id(1)

    @pl.when(t == 0)
    def _():
        s_ref[...] = s0_ref[...]

    tb = proj_ref.shape[0]
    qk = heads * dk
    v_off = 2 * qk
    r_off = v_off + heads * dv
    a_off = r_off + heads * dv
    scale = dk ** -0.5
    row = lax.broadcasted_iota(jnp.int32, (tb, 1), 0)
    zero_k = jnp.zeros((lc, dk), BF16)

    def blockdiag(kk):
        return jnp.concatenate(
            [jnp.concatenate([kk[:, h * dk:(h + 1) * dk] if g == h else zero_k
                              for g in range(heads)], axis=1) for h in range(heads)], axis=0)

    def block_rows(x, period, r):
        return jnp.concatenate(
            [jnp.broadcast_to(x[m:m + 1, :], (period, x.shape[1]))
             for m in range(r, x.shape[0], period)], axis=0)

    def level_exponent(h, b, lg):
        if h == 1:
            return jnp.where((row & 1) == 1, lg, 0.0)
        if h == 2:
            r4 = row & 3
            lg_next = pltpu.roll(lg, tb - 1, 0)
            lg_prev = pltpu.roll(lg, 1, 0)
            return jnp.where(r4 == 0, lg_next,
                             jnp.where(r4 == 1, 0.0, jnp.where(r4 == 2, lg, lg_prev + lg)))
        return -jnp.abs(b - block_rows(b, 2 * h, h - 1))

    nc = tb // lc
    chunks = [slice(c * lc, (c + 1) * lc) for c in range(nc)]
    q = proj_ref[:, 0:qk].astype(F32)
    k = proj_ref[:, qk:2 * qk].astype(F32) * scale
    z = _dot(proj_ref[:, a_off:a_off + LANES], wa2_ref[...]) + ba_ref[...]
    lg = (jnp.minimum(z, 0.0) - jnp.log(1.0 + jnp.exp(-jnp.abs(z)))) * (1.0 / GLA_TAU)
    lg_hi = lg.astype(BF16)
    lg_lo = (lg - lg_hi.astype(F32)).astype(BF16)
    tri = tri_ref[...]
    b = jnp.concatenate([_dot(tri, lg_hi[rs]) + _dot(tri, lg_lo[rs]) for rs in chunks], axis=0)
    b_last = block_rows(b, lc, lc - 1)
    qx = (q * jnp.exp(b)).astype(BF16)
    kx = (k * jnp.exp(b_last - b)).astype(BF16)
    f_last = jnp.exp(b_last)

    qb, kb = q.astype(BF16), k.astype(BF16)
    p = [masks_ref[nlev] * _dot_nt(qb[rs], blockdiag(kb[rs])) for rs in chunks]
    for lev in range(nlev):
        h = lc >> (lev + 1)
        second = ((row // h) & 1) == 1
        m = (jnp.where(second, q, k) * jnp.exp(level_exponent(h, b, lg))).astype(BF16)
        for c, rs in enumerate(chunks):
            p[c] = p[c] + masks_ref[lev] * _dot_nt(m[rs], blockdiag(m[rs]))

    v = [[proj_ref[rs, v_off + h * dv:v_off + (h + 1) * dv] for h in range(heads)]
         for rs in chunks]
    ds = [[_dot_tn(kx[rs, h * dk:(h + 1) * dk], v[c][h]) for h in range(heads)]
          for c, rs in enumerate(chunks)]

    s = [s_ref[h] for h in range(heads)]
    for c, rs in enumerate(chunks):
        p_bf = p[c].astype(BF16)
        for h in range(heads):
            ks = slice(h * dk, (h + 1) * dk)
            g0 = (h * lc) // LANES * LANES
            off = h * lc - g0
            lhs = jnp.concatenate([qx[rs, ks], p_bf[:, g0:g0 + LANES]], axis=1)
            rhs = [s[h].astype(BF16)]
            if off:
                rhs.append(jnp.zeros((off, dv), BF16))
            rhs.append(v[c][h])
            if LANES - off - lc:
                rhs.append(jnp.zeros((LANES - off - lc, dv), BF16))
            o = _dot(lhs, jnp.concatenate(rhs, axis=0))
            fl = f_last[c * lc:c * lc + 1, ks]
            decay = jnp.broadcast_to(fl, (dk, dk)).T
            s[h] = s[h] * jnp.concatenate([decay] * (dv // dk), axis=1) + ds[c][h]
            r = proj_ref[rs, r_off + h * dv:r_off + (h + 1) * dv].astype(F32)
            ms = jnp.mean(o * o, axis=-1, keepdims=True)
            on = o * lax.rsqrt(ms + EPS) * ng_ref[:, h * dv:(h + 1) * dv]
            y_scr[rs, h * dv:(h + 1) * dv] = (_silu(r) * on).astype(BF16)
    for h in range(heads):
        s_ref[h] = s[h]

    xo_ref[...] = x_ref[...] + _dot(y_scr[...], wout_ref[...])


def _gla(proj, x, s0, wa2p, ba, ng, wout, tb, lc):
    b, t, d = x.shape
    heads, dk, dv = s0.shape[1:]
    n = proj.shape[-1]
    tri, masks, nlev = _gla_tables(lc, heads)
    kern = functools.partial(_gla_kernel, lc=lc, nlev=nlev, heads=heads, dk=dk, dv=dv)
    return pl.pallas_call(
        kern,
        grid=(b, t // tb),
        in_specs=[
            pl.BlockSpec((None, tb, n), lambda i, j: (i, j, 0)),
            pl.BlockSpec((None, tb, d), lambda i, j: (i, j, 0)),
            pl.BlockSpec((None, heads, dk, dv), lambda i, j: (i, 0, 0, 0)),
            _const_spec(wa2p.shape),
            _const_spec((1, heads * dk)),
            _const_spec(tri.shape),
            _const_spec(masks.shape),
            _const_spec((1, heads * dv)),
            _const_spec((heads * dv, d)),
        ],
        out_specs=[
            pl.BlockSpec((None, tb, d), lambda i, j: (i, j, 0)),
            pl.BlockSpec((None, heads, dk, dv), lambda i, j: (i, 0, 0, 0)),
        ],
        out_shape=[
            jax.ShapeDtypeStruct((b, t, d), F32),
            jax.ShapeDtypeStruct(s0.shape, F32),
        ],
        scratch_shapes=[pltpu.VMEM((tb, heads * dv), BF16)],
        compiler_params=pltpu.CompilerParams(
            dimension_semantics=("parallel", "arbitrary"), vmem_limit_bytes=VMEM_LIMIT_BYTES),
        name="gla",
    )(proj, x, s0, wa2p, ba, jnp.asarray(tri, BF16), jnp.asarray(masks), ng, wout)


def _ffn_kernel(x_hbm, g_ref, wup_ref, cw_ref, cb_ref, wdn_ref, cs_ref, gf_ref,
                xo_hbm, nc_ref, xbuf, obuf, hbuf, ubuf, abuf, in_sem, out_sem,
                *, cwid, final_norm, seq_slabs):
    bi, t = pl.program_id(0), pl.program_id(1)
    nt = pl.num_programs(1)
    nsteps = pl.num_programs(0) * nt
    step = bi * nt + t
    slot = step % 2
    lslab, nslab = xbuf.shape[1], xbuf.shape[2]
    tm = lslab * nslab
    dff = wdn_ref.shape[0]

    def slab_copies(to_vmem, b_idx, t_idx, sl):
        copies = []
        for s_i in range(nslab):
            ref = x_hbm if to_vmem else xo_hbm
            if seq_slabs:
                hbm = ref.at[s_i]
            else:
                hbm = ref.at[b_idx, pl.ds(t_idx * tm + s_i * lslab, lslab), :]
            if to_vmem:
                copies.append(pltpu.make_async_copy(hbm, xbuf.at[sl, :, s_i, :], in_sem.at[sl, s_i]))
            else:
                copies.append(pltpu.make_async_copy(obuf.at[sl, :, s_i, :], hbm, out_sem.at[sl, s_i]))
        return copies

    @pl.when(step == 0)
    def _():
        for cp in slab_copies(True, bi, t, slot):
            cp.start()

    @pl.when(step + 1 < nsteps)
    def _():
        nxt = step + 1
        for cp in slab_copies(True, nxt // nt, nxt % nt, 1 - slot):
            cp.start()

    if not seq_slabs:
        @pl.when(t == 0)
        def _():
            nc_ref[...] = cs_ref[...]

    for cp in slab_copies(True, bi, t, slot):
        cp.wait()

    @pl.when(step >= 2)
    def _():
        for cp in slab_copies(False, bi, t, slot):
            cp.wait()

    d = xbuf.shape[-1]
    hbuf[...] = _rms(xbuf[slot].reshape(tm, d), g_ref[...]).astype(BF16)
    hdr = 2 * SUBLANES
    sub = lax.broadcasted_iota(jnp.int32, (SUBLANES, 1), 0)

    def conv(c0, slot):
        cols = slice(c0, c0 + cwid)
        u = _dot(hbuf[...], wup_ref[:, cols])
        ub = ubuf.at[slot]
        if seq_slabs:
            for r in range(CONV_W - 1):
                ub[r * SUBLANES:(r + 1) * SUBLANES, :] = jnp.concatenate(
                    [cs_ref[s_i, r:r + 1, cols] for s_i in range(nslab)], axis=0)
                for s_i in range(nslab):
                    row = tm - (CONV_W - 1 - r) * SUBLANES + s_i
                    nc_ref[s_i, r:r + 1, cols] = u[row:row + 1, :]
        else:
            ub[0:SUBLANES, :] = jnp.where(sub == 0, nc_ref[0:1, cols],
                                          pltpu.roll(u[tm - 2 * SUBLANES:tm - SUBLANES, :], 1, 0))
            ub[SUBLANES:hdr, :] = jnp.where(sub == 0, nc_ref[1:2, cols],
                                            pltpu.roll(u[tm - SUBLANES:tm, :], 1, 0))
            nc_ref[0:1, cols] = u[tm - SUBLANES - 1:tm - SUBLANES, :]
            nc_ref[1:2, cols] = u[tm - 1:tm, :]
        ub[hdr:hdr + tm, :] = u
        return (cb_ref[:, cols] + ub[0:tm, :] * cw_ref[0:1, cols]
                + ub[SUBLANES:SUBLANES + tm, :] * cw_ref[1:2, cols] + u * cw_ref[2:3, cols])

    for c in range(dff // cwid):
        gate = conv(c * cwid, 2 * (c % 2))
        val = conv(dff + c * cwid, 2 * (c % 2) + 1)
        abuf[:, c * cwid:(c + 1) * cwid] = (_silu(gate) * val).astype(BF16)
    out = xbuf[slot].reshape(tm, d) + _dot(abuf[...], wdn_ref[...])
    if final_norm:
        out = _rms(out, gf_ref[...])
    obuf[slot] = out.reshape(lslab, nslab, d)
    for cp in slab_copies(False, bi, t, slot):
        cp.start()

    @pl.when(step == nsteps - 1)
    def _():
        for cp in slab_copies(False, bi, t, slot):
            cp.wait()

        @pl.when(step >= 1)
        def _():
            for cp in slab_copies(False, bi, t, 1 - slot):
                cp.wait()


def _conv_ffn(x, g, wup, cw, cb, wdn, cs, gf, layer, tm, final_norm, cwid=256):
    b, t, d = x.shape
    dff = wdn.shape[1]
    seq_slabs = b == SUBLANES and t < FFN_TILE
    if seq_slabs:
        tm = b * t
        grid = (1, 1)
        cs_spec = pl.BlockSpec((None, b, CONV_W - 1, 2 * dff), lambda i, j: (layer, 0, 0, 0))
        nc_spec = pl.BlockSpec((b, CONV_W - 1, 2 * dff), lambda i, j: (0, 0, 0))
    else:
        grid = (b, t // tm)
        cs_spec = pl.BlockSpec((None, None, CONV_W - 1, 2 * dff), lambda i, j: (layer, i, 0, 0))
        nc_spec = pl.BlockSpec((None, CONV_W - 1, 2 * dff), lambda i, j: (i, 0, 0))
    kern = functools.partial(_ffn_kernel, cwid=cwid, final_norm=final_norm, seq_slabs=seq_slabs)
    return pl.pallas_call(
        kern,
        grid=grid,
        in_specs=[
            pl.BlockSpec(memory_space=pl.ANY),
            _layer_spec((1, d), layer),
            _layer_spec((d, 2 * dff), layer),
            _layer_spec((CONV_W, 2 * dff), layer),
            _layer_spec((1, 2 * dff), layer),
            _layer_spec((dff, d), layer),
            cs_spec,
            _const_spec((1, d)),
        ],
        out_specs=[pl.BlockSpec(memory_space=pl.ANY), nc_spec],
        out_shape=[
            jax.ShapeDtypeStruct((b, t, d), F32),
            jax.ShapeDtypeStruct((b, CONV_W - 1, 2 * dff), F32),
        ],
        scratch_shapes=[
            pltpu.VMEM((2, tm // SUBLANES, SUBLANES, d), F32),
            pltpu.VMEM((2, tm // SUBLANES, SUBLANES, d), F32),
            pltpu.VMEM((tm, d), BF16),
            pltpu.VMEM((4, tm + 2 * SUBLANES, cwid), F32),
            pltpu.VMEM((tm, dff), BF16),
            pltpu.SemaphoreType.DMA((2, SUBLANES)),
            pltpu.SemaphoreType.DMA((2, SUBLANES)),
        ],
        compiler_params=pltpu.CompilerParams(
            dimension_semantics=("arbitrary", "arbitrary"), vmem_limit_bytes=VMEM_LIMIT_BYTES),
        name="conv_ffn",
    )(x, g, wup, cw, cb, wdn, cs, gf)


def _tile(t, pref):
    return pref if t % pref == 0 else t


def _trunk(x, pos0, ret_state, gla_state, conv_state, w):
    b, t, d = x.shape
    tm = _tile(b * t, PROJ_TILE)
    tb = _tile(t, RET_TILE)
    ret_lc = _tile(tb, RET_CHUNK)
    gla_tb = _tile(t, GLA_TILE)
    gla_lc = _tile(gla_tb, GLA_CHUNK)

    half = ret_state.shape[-2] // 2
    inv = ROPE_BASE ** (-jnp.arange(half, dtype=F32) / half)
    blk = min(t, 64)
    ang_a = (pos0 + blk * jnp.arange(t // blk, dtype=jnp.int32)).astype(F32)[:, None] * inv[None, :]
    ang_r = jnp.arange(blk, dtype=F32)[:, None] * inv[None, :]
    ca, sa = jnp.cos(ang_a)[:, None, :], jnp.sin(ang_a)[:, None, :]
    cr, sr = jnp.cos(ang_r)[None, :, :], jnp.sin(ang_r)[None, :, :]
    cos = (ca * cr - sa * sr).reshape(t, half)
    sin = (sa * cr + ca * sr).reshape(t, half)

    proj = _norm_proj(x.reshape(b * t, d), w["norm_mix"][0:1], w["ret_w_in"], tm)
    x, ret_s = _retention(proj.reshape(b, t, -1), x, cos, sin, ret_state, w["ret_gn_g"],
                          w["ret_w_out"], tb, ret_lc)
    ffn = (w["norm_ffn"], w["ffn_w_up"], w["ffn_conv_w"], w["ffn_conv_b"], w["ffn_w_down"],
           conv_state, w["norm_final"])
    x, conv0 = _conv_ffn(x, *ffn, 0, _tile(t, FFN_TILE), False)
    proj = _norm_proj(x.reshape(b * t, d), w["norm_mix"][1:2], w["gla_w_in"], tm)
    x, gla_s = _gla(proj.reshape(b, t, -1), x, gla_state, w["gla_w_a2"], w["gla_b_a"],
                    w["gla_norm_g"], w["gla_w_out"], gla_tb, gla_lc)
    x, conv1 = _conv_ffn(x, *ffn, 1, _tile(t, FFN_TILE), True)
    return x, ret_s[None], gla_s[None], jnp.stack([conv0, conv1])


def kernel(x_prompt, x_sample, state_ret, state_gla, cache_conv, norm_mix, norm_ffn, norm_final,
           ret_w_in, ret_gn_g, ret_w_out, gla_w_in, gla_w_a2, gla_b_a, gla_norm_g, gla_w_out,
           ffn_w_up, ffn_conv_w, ffn_conv_b, ffn_w_down):
    assert norm_mix.shape[0] == 2 and ret_w_in.shape[0] == 1 and gla_w_in.shape[0] == 1
    d = x_prompt.shape[-1]
    gla_qk = gla_w_a2.shape[-1]
    gla_v = gla_w_out.shape[1]
    n_main = 2 * gla_qk + 2 * gla_v
    gla_w_in_p = jnp.concatenate(
        [gla_w_in[0, :, :n_main],
         jnp.pad(gla_w_in[0, :, n_main:], ((0, 0), (0, LANES - GLA_GATE_RANK)))], axis=1)
    w = dict(
        norm_mix=norm_mix, norm_ffn=norm_ffn[:, None, :], norm_final=norm_final.reshape(1, d),
        ret_w_in=ret_w_in[0].astype(BF16),
        ret_gn_g=ret_gn_g[0].reshape(1, -1),
        ret_w_out=ret_w_out[0].astype(BF16),
        gla_w_in=gla_w_in_p.astype(BF16),
        gla_w_a2=jnp.pad(gla_w_a2[0], ((0, LANES - GLA_GATE_RANK), (0, 0))).astype(BF16),
        gla_b_a=gla_b_a[0].reshape(1, -1),
        gla_norm_g=gla_norm_g[0].reshape(1, -1),
        gla_w_out=gla_w_out[0].astype(BF16),
        ffn_w_up=ffn_w_up.astype(BF16), ffn_conv_w=ffn_conv_w, ffn_conv_b=ffn_conv_b[:, None, :],
        ffn_w_down=ffn_w_down.astype(BF16),
    )
    bp = x_prompt.shape[0]
    ret0 = jnp.zeros((bp,) + state_ret.shape[2:], F32)
    gla0 = jnp.zeros((bp,) + state_gla.shape[2:], F32)
    conv0 = jnp.zeros((cache_conv.shape[0], bp) + cache_conv.shape[2:], F32)
    y_p, ret_p, gla_p, conv_p = _trunk(x_prompt, 0, ret0, gla0, conv0, w)
    y_s, ret_s, gla_s, conv_s = _trunk(x_sample, PAST_LEN, state_ret[0], state_gla[0],
                                       cache_conv, w)
    return (y_p, y_s, ret_p, ret_s, gla_p, gla_s, conv_p, conv_s)
```

```python
import functools
import math

import numpy as np
import jax
import jax.numpy as jnp
from jax import lax
from jax.experimental import pallas as pl
from jax.experimental.pallas import tpu as pltpu

EPS = 1e-6
ROPE_BASE = 10000.0
GLA_TAU = 16.0
GLA_GATE_RANK = 16
CONV_W = 3
PAST_LEN = 2048
GLA_CHUNK = 64
RET_CHUNK = 256
PROJ_TILE = 1024
RET_TILE = 512
GLA_TILE = 1024
FFN_TILE = 512
LANES = 128
SUBLANES = 8
VMEM_LIMIT_BYTES = 56 * 1024 * 1024

F32 = jnp.float32
BF16 = jnp.bfloat16


def _const_spec(shape):
    nd = len(shape)
    return pl.BlockSpec(shape, lambda *_: (0,) * nd, pipeline_mode=pl.Buffered(1))


def _layer_spec(shape, layer):
    nd = len(shape)
    return pl.BlockSpec((None,) + tuple(shape), lambda *_: (layer,) + (0,) * nd,
                        pipeline_mode=pl.Buffered(1))


def _rms(x, g):
    ms = jnp.mean(x * x, axis=-1, keepdims=True)
    return x * lax.rsqrt(ms + EPS) * g


def _silu(x):
    return x * (1.0 / (1.0 + jnp.exp(-x)))


def _dot(a, b):
    return jnp.dot(a, b, preferred_element_type=F32)


def _dot_nt(a, b):
    return lax.dot_general(a, b, (((1,), (1,)), ((), ())), preferred_element_type=F32)


def _dot_tn(a, b):
    return lax.dot_general(a, b, (((0,), (0,)), ((), ())), preferred_element_type=F32)


def _norm_proj_kernel(x_ref, g_ref, w_ref, o_ref, *, tn):
    h = _rms(x_ref[...], g_ref[...]).astype(BF16)
    n = w_ref.shape[1]
    for n0 in range(0, n, tn):
        n1 = min(n0 + tn, n)
        o_ref[:, n0:n1] = _dot(h, w_ref[:, n0:n1]).astype(o_ref.dtype)


def _norm_proj(x2, g, w, tm, tn=512):
    m, d = x2.shape
    n = w.shape[1]
    return pl.pallas_call(
        functools.partial(_norm_proj_kernel, tn=tn),
        grid=(m // tm,),
        in_specs=[
            pl.BlockSpec((tm, d), lambda i: (i, 0)),
            _const_spec((1, d)),
            _const_spec((d, n)),
        ],
        out_specs=pl.BlockSpec((tm, n), lambda i: (i, 0)),
        out_shape=jax.ShapeDtypeStruct((m, n), BF16),
        compiler_params=pltpu.CompilerParams(
            dimension_semantics=("parallel",), vmem_limit_bytes=VMEM_LIMIT_BYTES),
        name="norm_proj",
    )(x2, g, w)


def _ret_kernel(proj_ref, x_ref, cos_ref, sin_ref, s0_ref, dmat_ref, gn_ref, wout_ref,
                xo_ref, s_ref, y_scr, *, lc, heads, dk, dv, log_gamma):
    t = pl.program_id(1)

    @pl.when(t == 0)
    def _():
        s_ref[...] = s0_ref[...]

    tb = proj_ref.shape[0]
    half = dk // 2
    k_off = heads * dk
    v_off = 2 * heads * dk
    g_off = v_off + heads * dv
    scale = dk ** -0.5
    idx = lax.broadcasted_iota(jnp.int32, (lc, 1), 0).astype(F32)

    def rot(u, cos, sin):
        u1, u2 = u[:, :half], u[:, half:]
        return jnp.concatenate([u1 * cos - u2 * sin, u1 * sin + u2 * cos], axis=-1)

    for c in range(tb // lc):
        r0 = c * lc
        cos = cos_ref[r0:r0 + lc, :]
        sin = sin_ref[r0:r0 + lc, :]
        for h in range(heads):
            lg = log_gamma[h]
            q = rot(proj_ref[r0:r0 + lc, h * dk:(h + 1) * dk].astype(F32), cos, sin)
            k = rot(proj_ref[r0:r0 + lc, k_off + h * dk:k_off + (h + 1) * dk].astype(F32),
                    cos, sin) * scale
            v = proj_ref[r0:r0 + lc, v_off + h * dv:v_off + (h + 1) * dv]
            g = proj_ref[r0:r0 + lc, g_off + h * dv:g_off + (h + 1) * dv].astype(F32)
            s_old = s_ref[h]
            scores = _dot_nt(q.astype(BF16), k.astype(BF16)) * dmat_ref[h]
            o = _dot(scores.astype(BF16), v)
            q_dec = jnp.exp(lg * (idx + 1.0))
            o = o + _dot((q * q_dec).astype(BF16), s_old.astype(BF16))
            k_dec = jnp.exp(lg * (lc - 1.0 - idx))
            s_ref[h] = math.exp(lg * lc) * s_old + _dot_tn((k * k_dec).astype(BF16), v)
            mu = jnp.mean(o, axis=-1, keepdims=True)
            d = o - mu
            var = jnp.mean(d * d, axis=-1, keepdims=True)
            on = d * lax.rsqrt(var + EPS) * gn_ref[:, h * dv:(h + 1) * dv]
            y_scr[r0:r0 + lc, h * dv:(h + 1) * dv] = (_silu(g) * on).astype(BF16)

    xo_ref[...] = x_ref[...] + _dot(y_scr[...], wout_ref[...])


def _retention(proj, x, cos, sin, s0, gn, wout, tb, lc):
    b, t, d = x.shape
    heads, dk, dv = s0.shape[1:]
    n = proj.shape[-1]
    log_gamma = [math.log1p(-(2.0 ** (-5.0 - h))) for h in range(heads)]
    ii = np.arange(lc, dtype=np.float64)
    diff = ii[:, None] - ii[None, :]
    dmat = np.stack([np.where(diff >= 0, np.exp(lg * np.maximum(diff, 0.0)), 0.0)
                     for lg in log_gamma]).astype(np.float32)
    kern = functools.partial(_ret_kernel, lc=lc, heads=heads, dk=dk, dv=dv,
                             log_gamma=tuple(log_gamma))
    return pl.pallas_call(
        kern,
        grid=(b, t // tb),
        in_specs=[
            pl.BlockSpec((None, tb, n), lambda i, j: (i, j, 0)),
            pl.BlockSpec((None, tb, d), lambda i, j: (i, j, 0)),
            pl.BlockSpec((tb, dk // 2), lambda i, j: (j, 0)),
            pl.BlockSpec((tb, dk // 2), lambda i, j: (j, 0)),
            pl.BlockSpec((None, heads, dk, dv), lambda i, j: (i, 0, 0, 0)),
            _const_spec((heads, lc, lc)),
            _const_spec((1, heads * dv)),
            _const_spec((heads * dv, d)),
        ],
        out_specs=[
            pl.BlockSpec((None, tb, d), lambda i, j: (i, j, 0)),
            pl.BlockSpec((None, heads, dk, dv), lambda i, j: (i, 0, 0, 0)),
        ],
        out_shape=[
            jax.ShapeDtypeStruct((b, t, d), F32),
            jax.ShapeDtypeStruct(s0.shape, F32),
        ],
        scratch_shapes=[pltpu.VMEM((tb, heads * dv), BF16)],
        compiler_params=pltpu.CompilerParams(
            dimension_semantics=("parallel", "arbitrary"), vmem_limit_bytes=VMEM_LIMIT_BYTES),
        name="retention",
    )(proj, x, cos, sin, s0, jnp.asarray(dmat), gn, wout)


def _gla_tables(lc, heads):
    nlev = int(math.log2(lc))
    i = np.arange(lc)[:, None]
    j = np.arange(lc)[None, :]
    masks = []
    for lev in range(nlev):
        h = lc >> (lev + 1)
        same = (i // (2 * h)) == (j // (2 * h))
        masks.append(same & ((i // h) % 2 == 1) & ((j // h) % 2 == 0))
    masks.append(i == j)
    masks = np.tile(np.stack(masks).astype(np.float32), (1, 1, heads))
    tri = (j <= i).astype(np.float32)
    return tri, masks, nlev


def _gla_kernel(proj_ref, x_ref, s0_ref, wa2_ref, ba_ref, tri_ref, masks_ref, ng_ref, wout_ref,
                xo_ref, s_ref, y_scr, *, lc, nlev, heads, dk, dv):
    t = pl.program_id(1)

    @pl.when(t == 0)
    def _():
        s_ref[...] = s0_ref[...]

    tb = proj_ref.shape[0]
    qk = heads * dk
    v_off = 2 * qk
    r_off = v_off + heads * dv
    a_off = r_off + heads * dv
    scale = dk ** -0.5
    row = lax.broadcasted_iota(jnp.int32, (tb, 1), 0)
    zero_k = jnp.zeros((lc, dk), BF16)

    def blockdiag(kk):
        return jnp.concatenate(
            [jnp.concatenate([kk[:, h * dk:(h + 1) * dk] if g == h else zero_k
                              for g in range(heads)], axis=1) for h in range(heads)], axis=0)

    def block_rows(x, period, r):
        return jnp.concatenate(
            [jnp.broadcast_to(x[m:m + 1, :], (period, x.shape[1]))
             for m in range(r, x.shape[0], period)], axis=0)

    def level_exponent(h, b, lg):
        if h == 1:
            return jnp.where((row & 1) == 1, lg, 0.0)
        if h == 2:
            r4 = row & 3
            lg_next = pltpu.roll(lg, tb - 1, 0)
            lg_prev = pltpu.roll(lg, 1, 0)
            return jnp.where(r4 == 0, lg_next,
                             jnp.where(r4 == 1, 0.0, jnp.where(r4 == 2, lg, lg_prev + lg)))
        return -jnp.abs(b - block_rows(b, 2 * h, h - 1))

    nc = tb // lc
    chunks = [slice(c * lc, (c + 1) * lc) for c in range(nc)]
    q = proj_ref[:, 0:qk].astype(F32)
    k = proj_ref[:, qk:2 * qk].astype(F32) * scale
    z = _dot(proj_ref[:, a_off:a_off + LANES], wa2_ref[...]) + ba_ref[...]
    lg = (jnp.minimum(z, 0.0) - jnp.log(1.0 + jnp.exp(-jnp.abs(z)))) * (1.0 / GLA_TAU)
    lg_hi = lg.astype(BF16)
    lg_lo = (lg - lg_hi.astype(F32)).astype(BF16)
    tri = tri_ref[...]
    b = jnp.concatenate([_dot(tri, lg_hi[rs]) + _dot(tri, lg_lo[rs]) for rs in chunks], axis=0)
    b_last = block_rows(b, lc, lc - 1)
    qx = (q * jnp.exp(b)).astype(BF16)
    kx = (k * jnp.exp(b_last - b)).astype(BF16)
    f_last = jnp.exp(b_last)

    qb, kb = q.astype(BF16), k.astype(BF16)
    p = [masks_ref[nlev] * _dot_nt(qb[rs], blockdiag(kb[rs])) for rs in chunks]
    for lev in range(nlev):
        h = lc >> (lev + 1)
        second = ((row // h) & 1) == 1
        m = (jnp.where(second, q, k) * jnp.exp(level_exponent(h, b, lg))).astype(BF16)
        for c, rs in enumerate(chunks):
            p[c] = p[c] + masks_ref[lev] * _dot_nt(m[rs], blockdiag(m[rs]))

    v = [[proj_ref[rs, v_off + h * dv:v_off + (h + 1) * dv] for h in range(heads)]
         for rs in chunks]
    ds = [[_dot_tn(kx[rs, h * dk:(h + 1) * dk], v[c][h]) for h in range(heads)]
          for c, rs in enumerate(chunks)]

    s = [s_ref[h] for h in range(heads)]
    for c, rs in enumerate(chunks):
        p_bf = p[c].astype(BF16)
        for h in range(heads):
            ks = slice(h * dk, (h + 1) * dk)
            g0 = (h * lc) // LANES * LANES
            off = h * lc - g0
            lhs = jnp.concatenate([qx[rs, ks], p_bf[:, g0:g0 + LANES]], axis=1)
            rhs = [s[h].astype(BF16)]
            if off:
                rhs.append(jnp.zeros((off, dv), BF16))
            rhs.append(v[c][h])
            if LANES - off - lc:
                rhs.append(jnp.zeros((LANES - off - lc, dv), BF16))
            o = _dot(lhs, jnp.concatenate(rhs, axis=0))
            fl = f_last[c * lc:c * lc + 1, ks]
            decay = jnp.broadcast_to(fl, (dk, dk)).T
            s[h] = s[h] * jnp.concatenate([decay] * (dv // dk), axis=1) + ds[c][h]
            r = proj_ref[rs, r_off + h * dv:r_off + (h + 1) * dv].astype(F32)
            ms = jnp.mean(o * o, axis=-1, keepdims=True)
            on = o * lax.rsqrt(ms + EPS) * ng_ref[:, h * dv:(h + 1) * dv]
            y_scr[rs, h * dv:(h + 1) * dv] = (_silu(r) * on).astype(BF16)
    for h in range(heads):
        s_ref[h] = s[h]

    xo_ref[...] = x_ref[...] + _dot(y_scr[...], wout_ref[...])


def _gla(proj, x, s0, wa2p, ba, ng, wout, tb, lc):
    b, t, d = x.shape
    heads, dk, dv = s0.shape[1:]
    n = proj.shape[-1]
    tri, masks, nlev = _gla_tables(lc, heads)
    kern = functools.partial(_gla_kernel, lc=lc, nlev=nlev, heads=heads, dk=dk, dv=dv)
    return pl.pallas_call(
        kern,
        grid=(b, t // tb),
        in_specs=[
            pl.BlockSpec((None, tb, n), lambda i, j: (i, j, 0)),
            pl.BlockSpec((None, tb, d), lambda i, j: (i, j, 0)),
            pl.BlockSpec((None, heads, dk, dv), lambda i, j: (i, 0, 0, 0)),
            _const_spec(wa2p.shape),
            _const_spec((1, heads * dk)),
            _const_spec(tri.shape),
            _const_spec(masks.shape),
            _const_spec((1, heads * dv)),
            _const_spec((heads * dv, d)),
        ],
        out_specs=[
            pl.BlockSpec((None, tb, d), lambda i, j: (i, j, 0)),
            pl.BlockSpec((None, heads, dk, dv), lambda i, j: (i, 0, 0, 0)),
        ],
        out_shape=[
            jax.ShapeDtypeStruct((b, t, d), F32),
            jax.ShapeDtypeStruct(s0.shape, F32),
        ],
        scratch_shapes=[pltpu.VMEM((tb, heads * dv), BF16)],
        compiler_params=pltpu.CompilerParams(
            dimension_semantics=("parallel", "arbitrary"), vmem_limit_bytes=VMEM_LIMIT_BYTES),
        name="gla",
    )(proj, x, s0, wa2p, ba, jnp.asarray(tri, BF16), jnp.asarray(masks), ng, wout)


def _ffn_kernel(x_hbm, g_ref, wup_ref, cw_ref, cb_ref, wdn_ref, cs_ref, gf_ref,
                xo_hbm, nc_ref, xbuf, obuf, hbuf, ubuf, abuf, in_sem, out_sem,
                *, cwid, final_norm, seq_slabs):
    bi, t = pl.program_id(0), pl.program_id(1)
    nt = pl.num_programs(1)
    nsteps = pl.num_programs(0) * nt
    step = bi * nt + t
    slot = step % 2
    lslab, nslab = xbuf.shape[1], xbuf.shape[2]
    tm = lslab * nslab
    dff = wdn_ref.shape[0]

    def slab_copies(to_vmem, b_idx, t_idx, sl):
        copies = []
        for s_i in range(nslab):
            ref = x_hbm if to_vmem else xo_hbm
            if seq_slabs:
                hbm = ref.at[s_i]
            else:
                hbm = ref.at[b_idx, pl.ds(t_idx * tm + s_i * lslab, lslab), :]
            if to_vmem:
                copies.append(pltpu.make_async_copy(hbm, xbuf.at[sl, :, s_i, :], in_sem.at[sl, s_i]))
            else:
                copies.append(pltpu.make_async_copy(obuf.at[sl, :, s_i, :], hbm, out_sem.at[sl, s_i]))
        return copies

    @pl.when(step == 0)
    def _():
        for cp in slab_copies(True, bi, t, slot):
            cp.start()

    @pl.when(step + 1 < nsteps)
    def _():
        nxt = step + 1
        for cp in slab_copies(True, nxt // nt, nxt % nt, 1 - slot):
            cp.start()

    if not seq_slabs:
        @pl.when(t == 0)
        def _():
            nc_ref[...] = cs_ref[...]

    for cp in slab_copies(True, bi, t, slot):
        cp.wait()

    @pl.when(step >= 2)
    def _():
        for cp in slab_copies(False, bi, t, slot):
            cp.wait()

    d = xbuf.shape[-1]
    hbuf[...] = _rms(xbuf[slot].reshape(tm, d), g_ref[...]).astype(BF16)
    hdr = 2 * SUBLANES
    sub = lax.broadcasted_iota(jnp.int32, (SUBLANES, 1), 0)

    def conv(c0, slot):
        cols = slice(c0, c0 + cwid)
        u = _dot(hbuf[...], wup_ref[:, cols])
        ub = ubuf.at[slot]
        if seq_slabs:
            for r in range(CONV_W - 1):
                ub[r * SUBLANES:(r + 1) * SUBLANES, :] = jnp.concatenate(
                    [cs_ref[s_i, r:r + 1, cols] for s_i in range(nslab)], axis=0)
                for s_i in range(nslab):
                    row = tm - (CONV_W - 1 - r) * SUBLANES + s_i
                    nc_ref[s_i, r:r + 1, cols] = u[row:row + 1, :]
        else:
            ub[0:SUBLANES, :] = jnp.where(sub == 0, nc_ref[0:1, cols],
                                          pltpu.roll(u[tm - 2 * SUBLANES:tm - SUBLANES, :], 1, 0))
            ub[SUBLANES:hdr, :] = jnp.where(sub == 0, nc_ref[1:2, cols],
                                            pltpu.roll(u[tm - SUBLANES:tm, :], 1, 0))
            nc_ref[0:1, cols] = u[tm - SUBLANES - 1:tm - SUBLANES, :]
            nc_ref[1:2, cols] = u[tm - 1:tm, :]
        ub[hdr:hdr + tm, :] = u
        return (cb_ref[:, cols] + ub[0:tm, :] * cw_ref[0:1, cols]
                + ub[SUBLANES:SUBLANES + tm, :] * cw_ref[1:2, cols] + u * cw_ref[2:3, cols])

    for c in range(dff // cwid):
        gate = conv(c * cwid, 2 * (c % 2))
        val = conv(dff + c * cwid, 2 * (c % 2) + 1)
        abuf[:, c * cwid:(c + 1) * cwid] = (_silu(gate) * val).astype(BF16)
    out = xbuf[slot].reshape(tm, d) + _dot(abuf[...], wdn_ref[...])
    if final_norm:
        out = _rms(out, gf_ref[...])
    obuf[slot] = out.reshape(lslab, nslab, d)
    for cp in slab_copies(False, bi, t, slot):
        cp.start()

    @pl.when(step == nsteps - 1)
    def _():
        for cp in slab_copies(False, bi, t, slot):
            cp.wait()

        @pl.when(step >= 1)
        def _():
            for cp in slab_copies(False, bi, t, 1 - slot):
                cp.wait()


def _conv_ffn(x, g, wup, cw, cb, wdn, cs, gf, layer, tm, final_norm, cwid=256):
    b, t, d = x.shape
    dff = wdn.shape[1]
    seq_slabs = b == SUBLANES and t < FFN_TILE
    if seq_slabs:
        tm = b * t
        grid = (1, 1)
        cs_spec = pl.BlockSpec((None, b, CONV_W - 1, 2 * dff), lambda i, j: (layer, 0, 0, 0))
        nc_spec = pl.BlockSpec((b, CONV_W - 1, 2 * dff), lambda i, j: (0, 0, 0))
    else:
        grid = (b, t // tm)
        cs_spec = pl.BlockSpec((None, None, CONV_W - 1, 2 * dff), lambda i, j: (layer, i, 0, 0))
        nc_spec = pl.BlockSpec((None, CONV_W - 1, 2 * dff), lambda i, j: (i, 0, 0))
    kern = functools.partial(_ffn_kernel, cwid=cwid, final_norm=final_norm, seq_slabs=seq_slabs)
    return pl.pallas_call(
        kern,
        grid=grid,
        in_specs=[
            pl.BlockSpec(memory_space=pl.ANY),
            _layer_spec((1, d), layer),
            _layer_spec((d, 2 * dff), layer),
            _layer_spec((CONV_W, 2 * dff), layer),
            _layer_spec((1, 2 * dff), layer),
            _layer_spec((dff, d), layer),
            cs_spec,
            _const_spec((1, d)),
        ],
        out_specs=[pl.BlockSpec(memory_space=pl.ANY), nc_spec],
        out_shape=[
            jax.ShapeDtypeStruct((b, t, d), F32),
            jax.ShapeDtypeStruct((b, CONV_W - 1, 2 * dff), F32),
        ],
        scratch_shapes=[
            pltpu.VMEM((2, tm // SUBLANES, SUBLANES, d), F32),
            pltpu.VMEM((2, tm // SUBLANES, SUBLANES, d), F32),
            pltpu.VMEM((tm, d), BF16),
            pltpu.VMEM((4, tm + 2 * SUBLANES, cwid), F32),
            pltpu.VMEM((tm, dff), BF16),
            pltpu.SemaphoreType.DMA((2, SUBLANES)),
            pltpu.SemaphoreType.DMA((2, SUBLANES)),
        ],
        compiler_params=pltpu.CompilerParams(
            dimension_semantics=("arbitrary", "arbitrary"), vmem_limit_bytes=VMEM_LIMIT_BYTES),
        name="conv_ffn",
    )(x, g, wup, cw, cb, wdn, cs, gf)


def _tile(t, pref):
    return pref if t % pref == 0 else t


def _trunk(x, pos0, ret_state, gla_state, conv_state, w):
    b, t, d = x.shape
    tm = _tile(b * t, PROJ_TILE)
    tb = _tile(t, RET_TILE)
    ret_lc = _tile(tb, RET_CHUNK)
    gla_tb = _tile(t, GLA_TILE)
    gla_lc = _tile(gla_tb, GLA_CHUNK)

    half = ret_state.shape[-2] // 2
    inv = ROPE_BASE ** (-jnp.arange(half, dtype=F32) / half)
    blk = min(t, 64)
    ang_a = (pos0 + blk * jnp.arange(t // blk, dtype=jnp.int32)).astype(F32)[:, None] * inv[None, :]
    ang_r = jnp.arange(blk, dtype=F32)[:, None] * inv[None, :]
    ca, sa = jnp.cos(ang_a)[:, None, :], jnp.sin(ang_a)[:, None, :]
    cr, sr = jnp.cos(ang_r)[None, :, :], jnp.sin(ang_r)[None, :, :]
    cos = (ca * cr - sa * sr).reshape(t, half)
    sin = (sa * cr + ca * sr).reshape(t, half)

    proj = _norm_proj(x.reshape(b * t, d), w["norm_mix"][0:1], w["ret_w_in"], tm)
    x, ret_s = _retention(proj.reshape(b, t, -1), x, cos, sin, ret_state, w["ret_gn_g"],
                          w["ret_w_out"], tb, ret_lc)
    ffn = (w["norm_ffn"], w["ffn_w_up"], w["ffn_conv_w"], w["ffn_conv_b"], w["ffn_w_down"],
           conv_state, w["norm_final"])
    x, conv0 = _conv_ffn(x, *ffn, 0, _tile(t, FFN_TILE), False)
    proj = _norm_proj(x.reshape(b * t, d), w["norm_mix"][1:2], w["gla_w_in"], tm)
    x, gla_s = _gla(proj.reshape(b, t, -1), x, gla_state, w["gla_w_a2"], w["gla_b_a"],
                    w["gla_norm_g"], w["gla_w_out"], gla_tb, gla_lc)
    x, conv1 = _conv_ffn(x, *ffn, 1, _tile(t, FFN_TILE), True)
    return x, ret_s[None], gla_s[None], jnp.stack([conv0, conv1])


def kernel(x_prompt, x_sample, state_ret, state_gla, cache_conv, norm_mix, norm_ffn, norm_final,
           ret_w_in, ret_gn_g, ret_w_out, gla_w_in, gla_w_a2, gla_b_a, gla_norm_g, gla_w_out,
           ffn_w_up, ffn_conv_w, ffn_conv_b, ffn_w_down):
    assert norm_mix.shape[0] == 2 and ret_w_in.shape[0] == 1 and gla_w_in.shape[0] == 1
    d = x_prompt.shape[-1]
    gla_qk = gla_w_a2.shape[-1]
    gla_v = gla_w_out.shape[1]
    n_main = 2 * gla_qk + 2 * gla_v
    gla_w_in_p = jnp.concatenate(
        [gla_w_in[0, :, :n_main],
         jnp.pad(gla_w_in[0, :, n_main:], ((0, 0), (0, LANES - GLA_GATE_RANK)))], axis=1)
    w = dict(
        norm_mix=norm_mix, norm_ffn=norm_ffn[:, None, :], norm_final=norm_final.reshape(1, d),
        ret_w_in=ret_w_in[0].astype(BF16),
        ret_gn_g=ret_gn_g[0].reshape(1, -1),
        ret_w_out=ret_w_out[0].astype(BF16),
        gla_w_in=gla_w_in_p.astype(BF16),
        gla_w_a2=jnp.pad(gla_w_a2[0], ((0, LANES - GLA_GATE_RANK), (0, 0))).astype(BF16),
        gla_b_a=gla_b_a[0].reshape(1, -1),
        gla_norm_g=gla_norm_g[0].reshape(1, -1),
        gla_w_out=gla_w_out[0].astype(BF16),
        ffn_w_up=ffn_w_up.astype(BF16), ffn_conv_w=ffn_conv_w, ffn_conv_b=ffn_conv_b[:, None, :],
        ffn_w_down=ffn_w_down.astype(BF16),
    )
    bp = x_prompt.shape[0]
    ret0 = jnp.zeros((bp,) + state_ret.shape[2:], F32)
    gla0 = jnp.zeros((bp,) + state_gla.shape[2:], F32)
    conv0 = jnp.zeros((cache_conv.shape[0], bp) + cache_conv.shape[2:], F32)
    y_p, ret_p, gla_p, conv_p = _trunk(x_prompt, 0, ret0, gla0, conv0, w)
    y_s, ret_s, gla_s, conv_s = _trunk(x_sample, PAST_LEN, state_ret[0], state_gla[0],
                                       cache_conv, w)
    return (y_p, y_s, ret_p, ret_s, gla_p, gla_s, conv_p, conv_s)
```

```python
import functools
import math

import numpy as np
import jax
import jax.numpy as jnp
from jax import lax
from jax.experimental import pallas as pl
from jax.experimental.pallas import tpu as pltpu

EPS = 1e-6
ROPE_BASE = 10000.0
GLA_TAU = 16.0
GLA_GATE_RANK = 16
CONV_W = 3
PAST_LEN = 2048
GLA_CHUNK = 64
RET_CHUNK = 256
PROJ_TILE = 1024
RET_TILE = 512
GLA_TILE = 1024
FFN_TILE = 1024
LANES = 128
SUBLANES = 8
VMEM_LIMIT_BYTES = 56 * 1024 * 1024

F32 = jnp.float32
BF16 = jnp.bfloat16


def _const_spec(shape):
    nd = len(shape)
    return pl.BlockSpec(shape, lambda *_: (0,) * nd, pipeline_mode=pl.Buffered(1))


def _layer_spec(shape, layer):
    nd = len(shape)
    return pl.BlockSpec((None,) + tuple(shape), lambda *_: (layer,) + (0,) * nd,
                        pipeline_mode=pl.Buffered(1))


def _rms(x, g):
    ms = jnp.mean(x * x, axis=-1, keepdims=True)
    return x * lax.rsqrt(ms + EPS) * g


def _silu(x):
    return x * (1.0 / (1.0 + jnp.exp(-x)))


def _dot(a, b):
    return jnp.dot(a, b, preferred_element_type=F32)


def _dot_nt(a, b):
    return lax.dot_general(a, b, (((1,), (1,)), ((), ())), preferred_element_type=F32)


def _dot_tn(a, b):
    return lax.dot_general(a, b, (((0,), (0,)), ((), ())), preferred_element_type=F32)


def _norm_proj_kernel(x_ref, g_ref, w_ref, o_ref, *, tn):
    h = _rms(x_ref[...], g_ref[...]).astype(BF16)
    n = w_ref.shape[1]
    for n0 in range(0, n, tn):
        n1 = min(n0 + tn, n)
        o_ref[:, n0:n1] = _dot(h, w_ref[:, n0:n1]).astype(o_ref.dtype)


def _norm_proj(x2, g, w, tm, tn=512):
    m, d = x2.shape
    n = w.shape[1]
    return pl.pallas_call(
        functools.partial(_norm_proj_kernel, tn=tn),
        grid=(m // tm,),
        in_specs=[
            pl.BlockSpec((tm, d), lambda i: (i, 0)),
            _const_spec((1, d)),
            _const_spec((d, n)),
        ],
        out_specs=pl.BlockSpec((tm, n), lambda i: (i, 0)),
        out_shape=jax.ShapeDtypeStruct((m, n), BF16),
        compiler_params=pltpu.CompilerParams(
            dimension_semantics=("parallel",), vmem_limit_bytes=VMEM_LIMIT_BYTES),
        name="norm_proj",
    )(x2, g, w)


def _ret_kernel(proj_ref, x_ref, cos_ref, sin_ref, s0_ref, dmat_ref, gn_ref, wout_ref,
                xo_ref, s_ref, y_scr, *, lc, heads, dk, dv, log_gamma):
    t = pl.program_id(1)

    @pl.when(t == 0)
    def _():
        s_ref[...] = s0_ref[...]

    tb = proj_ref.shape[0]
    half = dk // 2
    k_off = heads * dk
    v_off = 2 * heads * dk
    g_off = v_off + heads * dv
    scale = dk ** -0.5
    idx = lax.broadcasted_iota(jnp.int32, (lc, 1), 0).astype(F32)

    def rot(u, cos, sin):
        u1, u2 = u[:, :half], u[:, half:]
        return jnp.concatenate([u1 * cos - u2 * sin, u1 * sin + u2 * cos], axis=-1)

    for c in range(tb // lc):
        r0 = c * lc
        cos = cos_ref[r0:r0 + lc, :]
        sin = sin_ref[r0:r0 + lc, :]
        for h in range(heads):
            lg = log_gamma[h]
            q = rot(proj_ref[r0:r0 + lc, h * dk:(h + 1) * dk].astype(F32), cos, sin)
            k = rot(proj_ref[r0:r0 + lc, k_off + h * dk:k_off + (h + 1) * dk].astype(F32),
                    cos, sin) * scale
            v = proj_ref[r0:r0 + lc, v_off + h * dv:v_off + (h + 1) * dv]
            g = proj_ref[r0:r0 + lc, g_off + h * dv:g_off + (h + 1) * dv].astype(F32)
            s_old = s_ref[h]
            scores = _dot_nt(q.astype(BF16), k.astype(BF16)) * dmat_ref[h]
            o = _dot(scores.astype(BF16), v)
            q_dec = jnp.exp(lg * (idx + 1.0))
            o = o + _dot((q * q_dec).astype(BF16), s_old.astype(BF16))
            k_dec = jnp.exp(lg * (lc - 1.0 - idx))
            s_ref[h] = math.exp(lg * lc) * s_old + _dot_tn((k * k_dec).astype(BF16), v)
            mu = jnp.mean(o, axis=-1, keepdims=True)
            d = o - mu
            var = jnp.mean(d * d, axis=-1, keepdims=True)
            on = d * lax.rsqrt(var + EPS) * gn_ref[:, h * dv:(h + 1) * dv]
            y_scr[r0:r0 + lc, h * dv:(h + 1) * dv] = (_silu(g) * on).astype(BF16)

    xo_ref[...] = x_ref[...] + _dot(y_scr[...], wout_ref[...])


def _retention(proj, x, cos, sin, s0, gn, wout, tb, lc):
    b, t, d = x.shape
    heads, dk, dv = s0.shape[1:]
    n = proj.shape[-1]
    log_gamma = [math.log1p(-(2.0 ** (-5.0 - h))) for h in range(heads)]
    ii = np.arange(lc, dtype=np.float64)
    diff = ii[:, None] - ii[None, :]
    dmat = np.stack([np.where(diff >= 0, np.exp(lg * np.maximum(diff, 0.0)), 0.0)
                     for lg in log_gamma]).astype(np.float32)
    kern = functools.partial(_ret_kernel, lc=lc, heads=heads, dk=dk, dv=dv,
                             log_gamma=tuple(log_gamma))
    return pl.pallas_call(
        kern,
        grid=(b, t // tb),
        in_specs=[
            pl.BlockSpec((None, tb, n), lambda i, j: (i, j, 0)),
            pl.BlockSpec((None, tb, d), lambda i, j: (i, j, 0)),
            pl.BlockSpec((tb, dk // 2), lambda i, j: (j, 0)),
            pl.BlockSpec((tb, dk // 2), lambda i, j: (j, 0)),
            pl.BlockSpec((None, heads, dk, dv), lambda i, j: (i, 0, 0, 0)),
            _const_spec((heads, lc, lc)),
            _const_spec((1, heads * dv)),
            _const_spec((heads * dv, d)),
        ],
        out_specs=[
            pl.BlockSpec((None, tb, d), lambda i, j: (i, j, 0)),
            pl.BlockSpec((None, heads, dk, dv), lambda i, j: (i, 0, 0, 0)),
        ],
        out_shape=[
            jax.ShapeDtypeStruct((b, t, d), F32),
            jax.ShapeDtypeStruct(s0.shape, F32),
        ],
        scratch_shapes=[pltpu.VMEM((tb, heads * dv), BF16)],
        compiler_params=pltpu.CompilerParams(
            dimension_semantics=("parallel", "arbitrary"), vmem_limit_bytes=VMEM_LIMIT_BYTES),
        name="retention",
    )(proj, x, cos, sin, s0, jnp.asarray(dmat), gn, wout)


def _gla_tables(lc, heads):
    nlev = int(math.log2(lc))
    i = np.arange(lc)[:, None]
    j = np.arange(lc)[None, :]
    masks = []
    for lev in range(nlev):
        h = lc >> (lev + 1)
        same = (i // (2 * h)) == (j // (2 * h))
        masks.append(same & ((i // h) % 2 == 1) & ((j // h) % 2 == 0))
    masks.append(i == j)
    masks = np.tile(np.stack(masks).astype(np.float32), (1, 1, heads))
    tri = (j <= i).astype(np.float32)
    return tri, masks, nlev


def _gla_kernel(proj_ref, x_ref, s0_ref, wa2_ref, ba_ref, tri_ref, masks_ref, ng_ref, wout_ref,
                xo_ref, s_ref, y_scr, *, lc, nlev, heads, dk, dv):
    t = pl.program_id(1)

    @pl.when(t == 0)
    def _():
        s_ref[...] = s0_ref[...]

    tb = proj_ref.shape[0]
    qk = heads * dk
    v_off = 2 * qk
    r_off = v_off + heads * dv
    a_off = r_off + heads * dv
    scale = dk ** -0.5
    row = lax.broadcasted_iota(jnp.int32, (tb, 1), 0)
    zero_k = jnp.zeros((lc, dk), BF16)

    def blockdiag(kk):
        return jnp.concatenate(
            [jnp.concatenate([kk[:, h * dk:(h + 1) * dk] if g == h else zero_k
                              for g in range(heads)], axis=1) for h in range(heads)], axis=0)

    def block_rows(x, period, r):
        return jnp.concatenate(
            [jnp.broadcast_to(x[m:m + 1, :], (period, x.shape[1]))
             for m in range(r, x.shape[0], period)], axis=0)

    def level_exponent(h, b, lg):
        if h == 1:
            return jnp.where((row & 1) == 1, lg, 0.0)
        if h == 2:
            r4 = row & 3
            lg_next = pltpu.roll(lg, tb - 1, 0)
            lg_prev = pltpu.roll(lg, 1, 0)
            return jnp.where(r4 == 0, lg_next,
                             jnp.where(r4 == 1, 0.0, jnp.where(r4 == 2, lg, lg_prev + lg)))
        return -jnp.abs(b - block_rows(b, 2 * h, h - 1))

    nc = tb // lc
    chunks = [slice(c * lc, (c + 1) * lc) for c in range(nc)]
    q = proj_ref[:, 0:qk].astype(F32)
    k = proj_ref[:, qk:2 * qk].astype(F32) * scale
    z = _dot(proj_ref[:, a_off:a_off + LANES], wa2_ref[...]) + ba_ref[...]
    lg = (jnp.minimum(z, 0.0) - jnp.log(1.0 + jnp.exp(-jnp.abs(z)))) * (1.0 / GLA_TAU)
    lg_hi = lg.astype(BF16)
    lg_lo = (lg - lg_hi.astype(F32)).astype(BF16)
    tri = tri_ref[...]
    b = jnp.concatenate([_dot(tri, lg_hi[rs]) + _dot(tri, lg_lo[rs]) for rs in chunks], axis=0)
    b_last = block_rows(b, lc, lc - 1)
    qx = (q * jnp.exp(b)).astype(BF16)
    kx = (k * jnp.exp(b_last - b)).astype(BF16)
    f_last = jnp.exp(b_last)

    qb, kb = q.astype(BF16), k.astype(BF16)
    p = [masks_ref[nlev] * _dot_nt(qb[rs], blockdiag(kb[rs])) for rs in chunks]
    for lev in range(nlev):
        h = lc >> (lev + 1)
        second = ((row // h) & 1) == 1
        m = (jnp.where(second, q, k) * jnp.exp(level_exponent(h, b, lg))).astype(BF16)
        for c, rs in enumerate(chunks):
            p[c] = p[c] + masks_ref[lev] * _dot_nt(m[rs], blockdiag(m[rs]))

    v = [[proj_ref[rs, v_off + h * dv:v_off + (h + 1) * dv] for h in range(heads)]
         for rs in chunks]
    ds = [[_dot_tn(kx[rs, h * dk:(h + 1) * dk], v[c][h]) for h in range(heads)]
          for c, rs in enumerate(chunks)]

    s = [s_ref[h] for h in range(heads)]
    for c, rs in enumerate(chunks):
        p_bf = p[c].astype(BF16)
        for h in range(heads):
            ks = slice(h * dk, (h + 1) * dk)
            g0 = (h * lc) // LANES * LANES
            off = h * lc - g0
            lhs = jnp.concatenate([qx[rs, ks], p_bf[:, g0:g0 + LANES]], axis=1)
            rhs = [s[h].astype(BF16)]
            if off:
                rhs.append(jnp.zeros((off, dv), BF16))
            rhs.append(v[c][h])
            if LANES - off - lc:
                rhs.append(jnp.zeros((LANES - off - lc, dv), BF16))
            o = _dot(lhs, jnp.concatenate(rhs, axis=0))
            fl = f_last[c * lc:c * lc + 1, ks]
            decay = jnp.broadcast_to(fl, (dk, dk)).T
            s[h] = s[h] * jnp.concatenate([decay] * (dv // dk), axis=1) + ds[c][h]
            r = proj_ref[rs, r_off + h * dv:r_off + (h + 1) * dv].astype(F32)
            ms = jnp.mean(o * o, axis=-1, keepdims=True)
            on = o * lax.rsqrt(ms + EPS) * ng_ref[:, h * dv:(h + 1) * dv]
            y_scr[rs, h * dv:(h + 1) * dv] = (_silu(r) * on).astype(BF16)
    for h in range(heads):
        s_ref[h] = s[h]

    xo_ref[...] = x_ref[...] + _dot(y_scr[...], wout_ref[...])


def _gla(proj, x, s0, wa2p, ba, ng, wout, tb, lc):
    b, t, d = x.shape
    heads, dk, dv = s0.shape[1:]
    n = proj.shape[-1]
    tri, masks, nlev = _gla_tables(lc, heads)
    kern = functools.partial(_gla_kernel, lc=lc, nlev=nlev, heads=heads, dk=dk, dv=dv)
    return pl.pallas_call(
        kern,
        grid=(b, t // tb),
        in_specs=[
            pl.BlockSpec((None, tb, n), lambda i, j: (i, j, 0)),
            pl.BlockSpec((None, tb, d), lambda i, j: (i, j, 0)),
            pl.BlockSpec((None, heads, dk, dv), lambda i, j: (i, 0, 0, 0)),
            _const_spec(wa2p.shape),
            _const_spec((1, heads * dk)),
            _const_spec(tri.shape),
            _const_spec(masks.shape),
            _const_spec((1, heads * dv)),
            _const_spec((heads * dv, d)),
        ],
        out_specs=[
            pl.BlockSpec((None, tb, d), lambda i, j: (i, j, 0)),
            pl.BlockSpec((None, heads, dk, dv), lambda i, j: (i, 0, 0, 0)),
        ],
        out_shape=[
            jax.ShapeDtypeStruct((b, t, d), F32),
            jax.ShapeDtypeStruct(s0.shape, F32),
        ],
        scratch_shapes=[pltpu.VMEM((tb, heads * dv), BF16)],
        compiler_params=pltpu.CompilerParams(
            dimension_semantics=("parallel", "arbitrary"), vmem_limit_bytes=VMEM_LIMIT_BYTES),
        name="gla",
    )(proj, x, s0, wa2p, ba, jnp.asarray(tri, BF16), jnp.asarray(masks), ng, wout)


def _ffn_kernel(x_hbm, g_ref, wup_ref, cw_ref, cb_ref, wdn_ref, cs_ref, gf_ref,
                xo_hbm, nc_ref, xbuf, obuf, hbuf, ubuf, abuf, in_sem, out_sem,
                *, cwid, final_norm, seq_slabs):
    bi, t = pl.program_id(0), pl.program_id(1)
    nt = pl.num_programs(1)
    nsteps = pl.num_programs(0) * nt
    step = bi * nt + t
    slot = step % 2
    lslab, nslab = xbuf.shape[1], xbuf.shape[2]
    tm = lslab * nslab
    dff = wdn_ref.shape[0]

    def slab_copies(to_vmem, b_idx, t_idx, sl):
        copies = []
        for s_i in range(nslab):
            ref = x_hbm if to_vmem else xo_hbm
            if seq_slabs:
                hbm = ref.at[s_i]
            else:
                hbm = ref.at[b_idx, pl.ds(t_idx * tm + s_i * lslab, lslab), :]
            if to_vmem:
                copies.append(pltpu.make_async_copy(hbm, xbuf.at[sl, :, s_i, :], in_sem.at[sl, s_i]))
            else:
                copies.append(pltpu.make_async_copy(obuf.at[sl, :, s_i, :], hbm, out_sem.at[sl, s_i]))
        return copies

    @pl.when(step == 0)
    def _():
        for cp in slab_copies(True, bi, t, slot):
            cp.start()

    @pl.when(step + 1 < nsteps)
    def _():
        nxt = step + 1
        for cp in slab_copies(True, nxt // nt, nxt % nt, 1 - slot):
            cp.start()

    if not seq_slabs:
        @pl.when(t == 0)
        def _():
            nc_ref[...] = cs_ref[...]

    for cp in slab_copies(True, bi, t, slot):
        cp.wait()

    @pl.when(step >= 2)
    def _():
        for cp in slab_copies(False, bi, t, slot):
            cp.wait()

    d = xbuf.shape[-1]
    hbuf[...] = _rms(xbuf[slot].reshape(tm, d), g_ref[...]).astype(BF16)
    hdr = 2 * SUBLANES
    sub = lax.broadcasted_iota(jnp.int32, (SUBLANES, 1), 0)

    def conv(c0, slot):
        cols = slice(c0, c0 + cwid)
        u = _dot(hbuf[...], wup_ref[:, cols])
        ub = ubuf.at[slot]
        if seq_slabs:
            for r in range(CONV_W - 1):
                ub[r * SUBLANES:(r + 1) * SUBLANES, :] = jnp.concatenate(
                    [cs_ref[s_i, r:r + 1, cols] for s_i in range(nslab)], axis=0)
                for s_i in range(nslab):
                    row = tm - (CONV_W - 1 - r) * SUBLANES + s_i
                    nc_ref[s_i, r:r + 1, cols] = u[row:row + 1, :]
        else:
            ub[0:SUBLANES, :] = jnp.where(sub == 0, nc_ref[0:1, cols],
                                          pltpu.roll(u[tm - 2 * SUBLANES:tm - SUBLANES, :], 1, 0))
            ub[SUBLANES:hdr, :] = jnp.where(sub == 0, nc_ref[1:2, cols],
                                            pltpu.roll(u[tm - SUBLANES:tm, :], 1, 0))
            nc_ref[0:1, cols] = u[tm - SUBLANES - 1:tm - SUBLANES, :]
            nc_ref[1:2, cols] = u[tm - 1:tm, :]
        ub[hdr:hdr + tm, :] = u
        return (cb_ref[:, cols] + ub[0:tm, :] * cw_ref[0:1, cols]
                + ub[SUBLANES:SUBLANES + tm, :] * cw_ref[1:2, cols] + u * cw_ref[2:3, cols])

    for c in range(dff // cwid):
        gate = conv(c * cwid, 2 * (c % 2))
        val = conv(dff + c * cwid, 2 * (c % 2) + 1)
        abuf[:, c * cwid:(c + 1) * cwid] = (_silu(gate) * val).astype(BF16)
    out = xbuf[slot].reshape(tm, d) + _dot(abuf[...], wdn_ref[...])
    if final_norm:
        out = _rms(out, gf_ref[...])
    obuf[slot] = out.reshape(lslab, nslab, d)
    for cp in slab_copies(False, bi, t, slot):
        cp.start()

    @pl.when(step == nsteps - 1)
    def _():
        for cp in slab_copies(False, bi, t, slot):
            cp.wait()

        @pl.when(step >= 1)
        def _():
            for cp in slab_copies(False, bi, t, 1 - slot):
                cp.wait()


def _conv_ffn(x, g, wup, cw, cb, wdn, cs, gf, layer, tm, final_norm, cwid=256):
    b, t, d = x.shape
    dff = wdn.shape[1]
    seq_slabs = b == SUBLANES and t < FFN_TILE
    if seq_slabs:
        tm = b * t
        grid = (1, 1)
        cs_spec = pl.BlockSpec((None, b, CONV_W - 1, 2 * dff), lambda i, j: (layer, 0, 0, 0))
        nc_spec = pl.BlockSpec((b, CONV_W - 1, 2 * dff), lambda i, j: (0, 0, 0))
    else:
        grid = (b, t // tm)
        cs_spec = pl.BlockSpec((None, None, CONV_W - 1, 2 * dff), lambda i, j: (layer, i, 0, 0))
        nc_spec = pl.BlockSpec((None, CONV_W - 1, 2 * dff), lambda i, j: (i, 0, 0))
    kern = functools.partial(_ffn_kernel, cwid=cwid, final_norm=final_norm, seq_slabs=seq_slabs)
    return pl.pallas_call(
        kern,
        grid=grid,
        in_specs=[
            pl.BlockSpec(memory_space=pl.ANY),
            _layer_spec((1, d), layer),
            _layer_spec((d, 2 * dff), layer),
            _layer_spec((CONV_W, 2 * dff), layer),
            _layer_spec((1, 2 * dff), layer),
            _layer_spec((dff, d), layer),
            cs_spec,
            _const_spec((1, d)),
        ],
        out_specs=[pl.BlockSpec(memory_space=pl.ANY), nc_spec],
        out_shape=[
            jax.ShapeDtypeStruct((b, t, d), F32),
            jax.ShapeDtypeStruct((b, CONV_W - 1, 2 * dff), F32),
        ],
        scratch_shapes=[
            pltpu.VMEM((2, tm // SUBLANES, SUBLANES, d), F32),
            pltpu.VMEM((2, tm // SUBLANES, SUBLANES, d), F32),
            pltpu.VMEM((tm, d), BF16),
            pltpu.VMEM((4, tm + 2 * SUBLANES, cwid), F32),
            pltpu.VMEM((tm, dff), BF16),
            pltpu.SemaphoreType.DMA((2, SUBLANES)),
            pltpu.SemaphoreType.DMA((2, SUBLANES)),
        ],
        compiler_params=pltpu.CompilerParams(
            dimension_semantics=("arbitrary", "arbitrary"), vmem_limit_bytes=VMEM_LIMIT_BYTES),
        name="conv_ffn",
    )(x, g, wup, cw, cb, wdn, cs, gf)


def _tile(t, pref):
    return pref if t % pref == 0 else t


def _trunk(x, pos0, ret_state, gla_state, conv_state, w):
    b, t, d = x.shape
    tm = _tile(b * t, PROJ_TILE)
    tb = _tile(t, RET_TILE)
    ret_lc = _tile(tb, RET_CHUNK)
    gla_tb = _tile(t, GLA_TILE)
    gla_lc = _tile(gla_tb, GLA_CHUNK)

    half = ret_state.shape[-2] // 2
    inv = ROPE_BASE ** (-jnp.arange(half, dtype=F32) / half)
    blk = min(t, 64)
    ang_a = (pos0 + blk * jnp.arange(t // blk, dtype=jnp.int32)).astype(F32)[:, None] * inv[None, :]
    ang_r = jnp.arange(blk, dtype=F32)[:, None] * inv[None, :]
    ca, sa = jnp.cos(ang_a)[:, None, :], jnp.sin(ang_a)[:, None, :]
    cr, sr = jnp.cos(ang_r)[None, :, :], jnp.sin(ang_r)[None, :, :]
    cos = (ca * cr - sa * sr).reshape(t, half)
    sin = (sa * cr + ca * sr).reshape(t, half)

    proj = _norm_proj(x.reshape(b * t, d), w["norm_mix"][0:1], w["ret_w_in"], tm)
    x, ret_s = _retention(proj.reshape(b, t, -1), x, cos, sin, ret_state, w["ret_gn_g"],
                          w["ret_w_out"], tb, ret_lc)
    ffn = (w["norm_ffn"], w["ffn_w_up"], w["ffn_conv_w"], w["ffn_conv_b"], w["ffn_w_down"],
           conv_state, w["norm_final"])
    x, conv0 = _conv_ffn(x, *ffn, 0, _tile(t, FFN_TILE), False)
    proj = _norm_proj(x.reshape(b * t, d), w["norm_mix"][1:2], w["gla_w_in"], tm)
    x, gla_s = _gla(proj.reshape(b, t, -1), x, gla_state, w["gla_w_a2"], w["gla_b_a"],
                    w["gla_norm_g"], w["gla_w_out"], gla_tb, gla_lc)
    x, conv1 = _conv_ffn(x, *ffn, 1, _tile(t, FFN_TILE), True)
    return x, ret_s[None], gla_s[None], jnp.stack([conv0, conv1])


def kernel(x_prompt, x_sample, state_ret, state_gla, cache_conv, norm_mix, norm_ffn, norm_final,
           ret_w_in, ret_gn_g, ret_w_out, gla_w_in, gla_w_a2, gla_b_a, gla_norm_g, gla_w_out,
           ffn_w_up, ffn_conv_w, ffn_conv_b, ffn_w_down):
    assert norm_mix.shape[0] == 2 and ret_w_in.shape[0] == 1 and gla_w_in.shape[0] == 1
    d = x_prompt.shape[-1]
    gla_qk = gla_w_a2.shape[-1]
    gla_v = gla_w_out.shape[1]
    n_main = 2 * gla_qk + 2 * gla_v
    gla_w_in_p = jnp.concatenate(
        [gla_w_in[0, :, :n_main],
         jnp.pad(gla_w_in[0, :, n_main:], ((0, 0), (0, LANES - GLA_GATE_RANK)))], axis=1)
    w = dict(
        norm_mix=norm_mix, norm_ffn=norm_ffn[:, None, :], norm_final=norm_final.reshape(1, d),
        ret_w_in=ret_w_in[0].astype(BF16),
        ret_gn_g=ret_gn_g[0].reshape(1, -1),
        ret_w_out=ret_w_out[0].astype(BF16),
        gla_w_in=gla_w_in_p.astype(BF16),
        gla_w_a2=jnp.pad(gla_w_a2[0], ((0, LANES - GLA_GATE_RANK), (0, 0))).astype(BF16),
        gla_b_a=gla_b_a[0].reshape(1, -1),
        gla_norm_g=gla_norm_g[0].reshape(1, -1),
        gla_w_out=gla_w_out[0].astype(BF16),
        ffn_w_up=ffn_w_up.astype(BF16), ffn_conv_w=ffn_conv_w, ffn_conv_b=ffn_conv_b[:, None, :],
        ffn_w_down=ffn_w_down.astype(BF16),
    )
    bp = x_prompt.shape[0]
    ret0 = jnp.zeros((bp,) + state_ret.shape[2:], F32)
    gla0 = jnp.zeros((bp,) + state_gla.shape[2:], F32)
    conv0 = jnp.zeros((cache_conv.shape[0], bp) + cache_conv.shape[2:], F32)
    y_p, ret_p, gla_p, conv_p = _trunk(x_prompt, 0, ret0, gla0, conv0, w)
    y_s, ret_s, gla_s, conv_s = _trunk(x_sample, PAST_LEN, state_ret[0], state_gla[0],
                                       cache_conv, w)
    return (y_p, y_s, ret_p, ret_s, gla_p, gla_s, conv_p, conv_s)
```

```python
import functools
import math

import numpy as np
import jax
import jax.numpy as jnp
from jax import lax
from jax.experimental import pallas as pl
from jax.experimental.pallas import tpu as pltpu

EPS = 1e-6
ROPE_BASE = 10000.0
GLA_TAU = 16.0
GLA_GATE_RANK = 16
CONV_W = 3
PAST_LEN = 2048
GLA_CHUNK = 64
RET_CHUNK = 256
PROJ_TILE = 1024
RET_TILE = 512
GLA_TILE = 1024
FFN_TILE = 1024
LANES = 128
SUBLANES = 8
VMEM_LIMIT_BYTES = 56 * 1024 * 1024

F32 = jnp.float32
BF16 = jnp.bfloat16


def _const_spec(shape):
    nd = len(shape)
    return pl.BlockSpec(shape, lambda *_: (0,) * nd, pipeline_mode=pl.Buffered(1))


def _layer_spec(shape, layer):
    nd = len(shape)
    return pl.BlockSpec((None,) + tuple(shape), lambda *_: (layer,) + (0,) * nd,
                        pipeline_mode=pl.Buffered(1))


def _rms(x, g):
    ms = jnp.mean(x * x, axis=-1, keepdims=True)
    return x * lax.rsqrt(ms + EPS) * g


def _silu(x):
    return x * (1.0 / (1.0 + jnp.exp(-x)))


def _dot(a, b):
    return jnp.dot(a, b, preferred_element_type=F32)


def _dot_nt(a, b):
    return lax.dot_general(a, b, (((1,), (1,)), ((), ())), preferred_element_type=F32)


def _dot_tn(a, b):
    return lax.dot_general(a, b, (((0,), (0,)), ((), ())), preferred_element_type=F32)


def _norm_proj_kernel(x_ref, g_ref, w_ref, o_ref, *, tn):
    h = _rms(x_ref[...], g_ref[...]).astype(BF16)
    n = w_ref.shape[1]
    for n0 in range(0, n, tn):
        n1 = min(n0 + tn, n)
        o_ref[:, n0:n1] = _dot(h, w_ref[:, n0:n1]).astype(o_ref.dtype)


def _norm_proj(x2, g, w, tm, tn=512):
    m, d = x2.shape
    n = w.shape[1]
    return pl.pallas_call(
        functools.partial(_norm_proj_kernel, tn=tn),
        grid=(m // tm,),
        in_specs=[
            pl.BlockSpec((tm, d), lambda i: (i, 0)),
            _const_spec((1, d)),
            _const_spec((d, n)),
        ],
        out_specs=pl.BlockSpec((tm, n), lambda i: (i, 0)),
        out_shape=jax.ShapeDtypeStruct((m, n), BF16),
        compiler_params=pltpu.CompilerParams(
            dimension_semantics=("parallel",), vmem_limit_bytes=VMEM_LIMIT_BYTES),
        name="norm_proj",
    )(x2, g, w)


def _ret_kernel(proj_ref, x_ref, cos_ref, sin_ref, s0_ref, dmat_ref, gn_ref, wout_ref,
                xo_ref, s_ref, y_scr, *, lc, heads, dk, dv, log_gamma):
    t = pl.program_id(1)

    @pl.when(t == 0)
    def _():
        s_ref[...] = s0_ref[...]

    tb = proj_ref.shape[0]
    half = dk // 2
    k_off = heads * dk
    v_off = 2 * heads * dk
    g_off = v_off + heads * dv
    scale = dk ** -0.5
    idx = lax.broadcasted_iota(jnp.int32, (lc, 1), 0).astype(F32)

    def rot(u, cos, sin):
        u1, u2 = u[:, :half], u[:, half:]
        return jnp.concatenate([u1 * cos - u2 * sin, u1 * sin + u2 * cos], axis=-1)

    for c in range(tb // lc):
        r0 = c * lc
        cos = cos_ref[r0:r0 + lc, :]
        sin = sin_ref[r0:r0 + lc, :]
        for h in range(heads):
            lg = log_gamma[h]
            q = rot(proj_ref[r0:r0 + lc, h * dk:(h + 1) * dk].astype(F32), cos, sin)
            k = rot(proj_ref[r0:r0 + lc, k_off + h * dk:k_off + (h + 1) * dk].astype(F32),
                    cos, sin)
            v = proj_ref[r0:r0 + lc, v_off + h * dv:v_off + (h + 1) * dv]
            g = proj_ref[r0:r0 + lc, g_off + h * dv:g_off + (h + 1) * dv].astype(F32)
            s_old = s_ref[h]
            scores = _dot_nt(q.astype(BF16), k.astype(BF16)) * dmat_ref[h]
            o = _dot(scores.astype(BF16), v)
            q_dec = jnp.exp(lg * (idx + 1.0))
            o = o + _dot((q * q_dec).astype(BF16), s_old.astype(BF16))
            k_dec = jnp.exp(lg * (lc - 1.0 - idx)) * scale
            s_ref[h] = math.exp(lg * lc) * s_old + _dot_tn((k * k_dec).astype(BF16), v)
            mu = jnp.mean(o, axis=-1, keepdims=True)
            d = o - mu
            var = jnp.mean(d * d, axis=-1, keepdims=True)
            on = d * lax.rsqrt(var + EPS) * gn_ref[:, h * dv:(h + 1) * dv]
            y_scr[r0:r0 + lc, h * dv:(h + 1) * dv] = (_silu(g) * on).astype(BF16)

    xo_ref[...] = x_ref[...] + _dot(y_scr[...], wout_ref[...])


def _retention(proj, x, cos, sin, s0, gn, wout, tb, lc):
    b, t, d = x.shape
    heads, dk, dv = s0.shape[1:]
    n = proj.shape[-1]
    log_gamma = [math.log1p(-(2.0 ** (-5.0 - h))) for h in range(heads)]
    ii = np.arange(lc, dtype=np.float64)
    diff = ii[:, None] - ii[None, :]
    dmat = np.stack([np.where(diff >= 0, np.exp(lg * np.maximum(diff, 0.0)), 0.0)
                     for lg in log_gamma]).astype(np.float32) * np.float32(dk ** -0.5)
    kern = functools.partial(_ret_kernel, lc=lc, heads=heads, dk=dk, dv=dv,
                             log_gamma=tuple(log_gamma))
    return pl.pallas_call(
        kern,
        grid=(b, t // tb),
        in_specs=[
            pl.BlockSpec((None, tb, n), lambda i, j: (i, j, 0)),
            pl.BlockSpec((None, tb, d), lambda i, j: (i, j, 0)),
            pl.BlockSpec((tb, dk // 2), lambda i, j: (j, 0)),
            pl.BlockSpec((tb, dk // 2), lambda i, j: (j, 0)),
            pl.BlockSpec((None, heads, dk, dv), lambda i, j: (i, 0, 0, 0)),
            _const_spec((heads, lc, lc)),
            _const_spec((1, heads * dv)),
            _const_spec((heads * dv, d)),
        ],
        out_specs=[
            pl.BlockSpec((None, tb, d), lambda i, j: (i, j, 0)),
            pl.BlockSpec((None, heads, dk, dv), lambda i, j: (i, 0, 0, 0)),
        ],
        out_shape=[
            jax.ShapeDtypeStruct((b, t, d), F32),
            jax.ShapeDtypeStruct(s0.shape, F32),
        ],
        scratch_shapes=[pltpu.VMEM((tb, heads * dv), BF16)],
        compiler_params=pltpu.CompilerParams(
            dimension_semantics=("parallel", "arbitrary"), vmem_limit_bytes=VMEM_LIMIT_BYTES),
        name="retention",
    )(proj, x, cos, sin, s0, jnp.asarray(dmat), gn, wout)


def _gla_tables(lc, heads):
    nlev = int(math.log2(lc))
    i = np.arange(lc)[:, None]
    j = np.arange(lc)[None, :]
    masks = []
    for lev in range(nlev):
        h = lc >> (lev + 1)
        same = (i // (2 * h)) == (j // (2 * h))
        masks.append(same & ((i // h) % 2 == 1) & ((j // h) % 2 == 0))
    masks.append(i == j)
    masks = np.tile(np.stack(masks).astype(np.float32), (1, 1, heads))
    tri = (j <= i).astype(np.float32)
    return tri, masks, nlev


def _gla_kernel(proj_ref, x_ref, s0_ref, wa2_ref, ba_ref, tri_ref, masks_ref, ng_ref, wout_ref,
                xo_ref, s_ref, y_scr, *, lc, nlev, heads, dk, dv):
    t = pl.program_id(1)

    @pl.when(t == 0)
    def _():
        s_ref[...] = s0_ref[...]

    tb = proj_ref.shape[0]
    qk = heads * dk
    v_off = 2 * qk
    r_off = v_off + heads * dv
    a_off = r_off + heads * dv
    scale = dk ** -0.5
    row = lax.broadcasted_iota(jnp.int32, (tb, 1), 0)
    zero_k = jnp.zeros((lc, dk), BF16)

    def blockdiag(kk):
        return jnp.concatenate(
            [jnp.concatenate([kk[:, h * dk:(h + 1) * dk] if g == h else zero_k
                              for g in range(heads)], axis=1) for h in range(heads)], axis=0)

    def block_rows(x, period, r):
        return jnp.concatenate(
            [jnp.broadcast_to(x[m:m + 1, :], (period, x.shape[1]))
             for m in range(r, x.shape[0], period)], axis=0)

    def level_exponent(h, b, lg):
        if h == 1:
            return jnp.where((row & 1) == 1, lg, 0.0)
        if h == 2:
            r4 = row & 3
            lg_next = pltpu.roll(lg, tb - 1, 0)
            lg_prev = pltpu.roll(lg, 1, 0)
            return jnp.where(r4 == 0, lg_next,
                             jnp.where(r4 == 1, 0.0, jnp.where(r4 == 2, lg, lg_prev + lg)))
        return -jnp.abs(b - block_rows(b, 2 * h, h - 1))

    nc = tb // lc
    chunks = [slice(c * lc, (c + 1) * lc) for c in range(nc)]
    q = proj_ref[:, 0:qk].astype(F32)
    k = proj_ref[:, qk:2 * qk].astype(F32) * scale
    z = _dot(proj_ref[:, a_off:a_off + LANES], wa2_ref[...]) + ba_ref[...]
    lg = (jnp.minimum(z, 0.0) - jnp.log(1.0 + jnp.exp(-jnp.abs(z)))) * (1.0 / GLA_TAU)
    lg_hi = lg.astype(BF16)
    lg_lo = (lg - lg_hi.astype(F32)).astype(BF16)
    tri = tri_ref[...]
    b = jnp.concatenate([_dot(tri, lg_hi[rs]) + _dot(tri, lg_lo[rs]) for rs in chunks], axis=0)
    b_last = block_rows(b, lc, lc - 1)
    qx = (q * jnp.exp(b)).astype(BF16)
    kx = (k * jnp.exp(b_last - b)).astype(BF16)
    f_last = jnp.exp(b_last)

    qb, kb = q.astype(BF16), k.astype(BF16)
    p = [masks_ref[nlev] * _dot_nt(qb[rs], blockdiag(kb[rs])) for rs in chunks]
    for lev in range(nlev):
        h = lc >> (lev + 1)
        second = ((row // h) & 1) == 1
        m = (jnp.where(second, q, k) * jnp.exp(level_exponent(h, b, lg))).astype(BF16)
        for c, rs in enumerate(chunks):
            p[c] = p[c] + masks_ref[lev] * _dot_nt(m[rs], blockdiag(m[rs]))

    v = [[proj_ref[rs, v_off + h * dv:v_off + (h + 1) * dv] for h in range(heads)]
         for rs in chunks]
    ds = [[_dot_tn(kx[rs, h * dk:(h + 1) * dk], v[c][h]) for h in range(heads)]
          for c, rs in enumerate(chunks)]

    s = [s_ref[h] for h in range(heads)]
    for c, rs in enumerate(chunks):
        p_bf = p[c].astype(BF16)
        for h in range(heads):
            ks = slice(h * dk, (h + 1) * dk)
            g0 = (h * lc) // LANES * LANES
            off = h * lc - g0
            lhs = jnp.concatenate([qx[rs, ks], p_bf[:, g0:g0 + LANES]], axis=1)
            rhs = [s[h].astype(BF16)]
            if off:
                rhs.append(jnp.zeros((off, dv), BF16))
            rhs.append(v[c][h])
            if LANES - off - lc:
                rhs.append(jnp.zeros((LANES - off - lc, dv), BF16))
            o = _dot(lhs, jnp.concatenate(rhs, axis=0))
            fl = f_last[c * lc:c * lc + 1, ks]
            decay = jnp.broadcast_to(fl, (dk, dk)).T
            s[h] = s[h] * jnp.concatenate([decay] * (dv // dk), axis=1) + ds[c][h]
            r = proj_ref[rs, r_off + h * dv:r_off + (h + 1) * dv].astype(F32)
            ms = jnp.mean(o * o, axis=-1, keepdims=True)
            on = o * lax.rsqrt(ms + EPS) * ng_ref[:, h * dv:(h + 1) * dv]
            y_scr[rs, h * dv:(h + 1) * dv] = (_silu(r) * on).astype(BF16)
    for h in range(heads):
        s_ref[h] = s[h]

    xo_ref[...] = x_ref[...] + _dot(y_scr[...], wout_ref[...])


def _gla(proj, x, s0, wa2p, ba, ng, wout, tb, lc):
    b, t, d = x.shape
    heads, dk, dv = s0.shape[1:]
    n = proj.shape[-1]
    tri, masks, nlev = _gla_tables(lc, heads)
    kern = functools.partial(_gla_kernel, lc=lc, nlev=nlev, heads=heads, dk=dk, dv=dv)
    return pl.pallas_call(
        kern,
        grid=(b, t // tb),
        in_specs=[
            pl.BlockSpec((None, tb, n), lambda i, j: (i, j, 0)),
            pl.BlockSpec((None, tb, d), lambda i, j: (i, j, 0)),
            pl.BlockSpec((None, heads, dk, dv), lambda i, j: (i, 0, 0, 0)),
            _const_spec(wa2p.shape),
            _const_spec((1, heads * dk)),
            _const_spec(tri.shape),
            _const_spec(masks.shape),
            _const_spec((1, heads * dv)),
            _const_spec((heads * dv, d)),
        ],
        out_specs=[
            pl.BlockSpec((None, tb, d), lambda i, j: (i, j, 0)),
            pl.BlockSpec((None, heads, dk, dv), lambda i, j: (i, 0, 0, 0)),
        ],
        out_shape=[
            jax.ShapeDtypeStruct((b, t, d), F32),
            jax.ShapeDtypeStruct(s0.shape, F32),
        ],
        scratch_shapes=[pltpu.VMEM((tb, heads * dv), BF16)],
        compiler_params=pltpu.CompilerParams(
            dimension_semantics=("parallel", "arbitrary"), vmem_limit_bytes=VMEM_LIMIT_BYTES),
        name="gla",
    )(proj, x, s0, wa2p, ba, jnp.asarray(tri, BF16), jnp.asarray(masks), ng, wout)


def _ffn_kernel(x_hbm, g_ref, wup_ref, cw_ref, cb_ref, wdn_ref, cs_ref, gf_ref,
                xo_hbm, nc_ref, xbuf, obuf, hbuf, ubuf, abuf, in_sem, out_sem,
                *, cwid, final_norm, seq_slabs):
    bi, t = pl.program_id(0), pl.program_id(1)
    nt = pl.num_programs(1)
    nsteps = pl.num_programs(0) * nt
    step = bi * nt + t
    slot = step % 2
    lslab, nslab = xbuf.shape[1], xbuf.shape[2]
    tm = lslab * nslab
    dff = wdn_ref.shape[0]

    def slab_copies(to_vmem, b_idx, t_idx, sl):
        copies = []
        for s_i in range(nslab):
            ref = x_hbm if to_vmem else xo_hbm
            if seq_slabs:
                hbm = ref.at[s_i]
            else:
                hbm = ref.at[b_idx, pl.ds(t_idx * tm + s_i * lslab, lslab), :]
            if to_vmem:
                copies.append(pltpu.make_async_copy(hbm, xbuf.at[sl, :, s_i, :], in_sem.at[sl, s_i]))
            else:
                copies.append(pltpu.make_async_copy(obuf.at[sl, :, s_i, :], hbm, out_sem.at[sl, s_i]))
        return copies

    @pl.when(step == 0)
    def _():
        for cp in slab_copies(True, bi, t, slot):
            cp.start()

    @pl.when(step + 1 < nsteps)
    def _():
        nxt = step + 1
        for cp in slab_copies(True, nxt // nt, nxt % nt, 1 - slot):
            cp.start()

    if not seq_slabs:
        @pl.when(t == 0)
        def _():
            nc_ref[...] = cs_ref[...]

    for cp in slab_copies(True, bi, t, slot):
        cp.wait()

    @pl.when(step >= 2)
    def _():
        for cp in slab_copies(False, bi, t, slot):
            cp.wait()

    d = xbuf.shape[-1]
    hbuf[...] = _rms(xbuf[slot].reshape(tm, d), g_ref[...]).astype(BF16)
    hdr = 2 * SUBLANES
    sub = lax.broadcasted_iota(jnp.int32, (SUBLANES, 1), 0)

    def conv(c0, slot):
        cols = slice(c0, c0 + cwid)
        u = _dot(hbuf[...], wup_ref[:, cols])
        ub = ubuf.at[slot]
        if seq_slabs:
            for r in range(CONV_W - 1):
                ub[r * SUBLANES:(r + 1) * SUBLANES, :] = jnp.concatenate(
                    [cs_ref[s_i, r:r + 1, cols] for s_i in range(nslab)], axis=0)
                for s_i in range(nslab):
                    row = tm - (CONV_W - 1 - r) * SUBLANES + s_i
                    nc_ref[s_i, r:r + 1, cols] = u[row:row + 1, :]
        else:
            ub[0:SUBLANES, :] = jnp.where(sub == 0, nc_ref[0:1, cols],
                                          pltpu.roll(u[tm - 2 * SUBLANES:tm - SUBLANES, :], 1, 0))
            ub[SUBLANES:hdr, :] = jnp.where(sub == 0, nc_ref[1:2, cols],
                                            pltpu.roll(u[tm - SUBLANES:tm, :], 1, 0))
            nc_ref[0:1, cols] = u[tm - SUBLANES - 1:tm - SUBLANES, :]
            nc_ref[1:2, cols] = u[tm - 1:tm, :]
        ub[hdr:hdr + tm, :] = u
        return (cb_ref[:, cols] + ub[0:tm, :] * cw_ref[0:1, cols]
                + ub[SUBLANES:SUBLANES + tm, :] * cw_ref[1:2, cols] + u * cw_ref[2:3, cols])

    for c in range(dff // cwid):
        gate = conv(c * cwid, 2 * (c % 2))
        val = conv(dff + c * cwid, 2 * (c % 2) + 1)
        abuf[:, c * cwid:(c + 1) * cwid] = (_silu(gate) * val).astype(BF16)
    out = xbuf[slot].reshape(tm, d) + _dot(abuf[...], wdn_ref[...])
    if final_norm:
        out = _rms(out, gf_ref[...])
    obuf[slot] = out.reshape(lslab, nslab, d)
    for cp in slab_copies(False, bi, t, slot):
        cp.start()

    @pl.when(step == nsteps - 1)
    def _():
        for cp in slab_copies(False, bi, t, slot):
            cp.wait()

        @pl.when(step >= 1)
        def _():
            for cp in slab_copies(False, bi, t, 1 - slot):
                cp.wait()


def _conv_ffn(x, g, wup, cw, cb, wdn, cs, gf, layer, tm, final_norm, cwid=256):
    b, t, d = x.shape
    dff = wdn.shape[1]
    seq_slabs = b == SUBLANES and t < FFN_TILE
    if seq_slabs:
        tm = b * t
        grid = (1, 1)
        cs_spec = pl.BlockSpec((None, b, CONV_W - 1, 2 * dff), lambda i, j: (layer, 0, 0, 0))
        nc_spec = pl.BlockSpec((b, CONV_W - 1, 2 * dff), lambda i, j: (0, 0, 0))
    else:
        grid = (b, t // tm)
        cs_spec = pl.BlockSpec((None, None, CONV_W - 1, 2 * dff), lambda i, j: (layer, i, 0, 0))
        nc_spec = pl.BlockSpec((None, CONV_W - 1, 2 * dff), lambda i, j: (i, 0, 0))
    kern = functools.partial(_ffn_kernel, cwid=cwid, final_norm=final_norm, seq_slabs=seq_slabs)
    return pl.pallas_call(
        kern,
        grid=grid,
        in_specs=[
            pl.BlockSpec(memory_space=pl.ANY),
            _layer_spec((1, d), layer),
            _layer_spec((d, 2 * dff), layer),
            _layer_spec((CONV_W, 2 * dff), layer),
            _layer_spec((1, 2 * dff), layer),
            _layer_spec((dff, d), layer),
            cs_spec,
            _const_spec((1, d)),
        ],
        out_specs=[pl.BlockSpec(memory_space=pl.ANY), nc_spec],
        out_shape=[
            jax.ShapeDtypeStruct((b, t, d), F32),
            jax.ShapeDtypeStruct((b, CONV_W - 1, 2 * dff), F32),
        ],
        scratch_shapes=[
            pltpu.VMEM((2, tm // SUBLANES, SUBLANES, d), F32),
            pltpu.VMEM((2, tm // SUBLANES, SUBLANES, d), F32),
            pltpu.VMEM((tm, d), BF16),
            pltpu.VMEM((4, tm + 2 * SUBLANES, cwid), F32),
            pltpu.VMEM((tm, dff), BF16),
            pltpu.SemaphoreType.DMA((2, SUBLANES)),
            pltpu.SemaphoreType.DMA((2, SUBLANES)),
        ],
        compiler_params=pltpu.CompilerParams(
            dimension_semantics=("arbitrary", "arbitrary"), vmem_limit_bytes=VMEM_LIMIT_BYTES),
        name="conv_ffn",
    )(x, g, wup, cw, cb, wdn, cs, gf)


def _tile(t, pref):
    return pref if t % pref == 0 else t


def _trunk(x, pos0, ret_state, gla_state, conv_state, w):
    b, t, d = x.shape
    tm = _tile(b * t, PROJ_TILE)
    tb = _tile(t, RET_TILE)
    ret_lc = _tile(tb, RET_CHUNK)
    gla_tb = _tile(t, GLA_TILE)
    gla_lc = _tile(gla_tb, GLA_CHUNK)

    half = ret_state.shape[-2] // 2
    inv = ROPE_BASE ** (-jnp.arange(half, dtype=F32) / half)
    blk = min(t, 64)
    ang_a = (pos0 + blk * jnp.arange(t // blk, dtype=jnp.int32)).astype(F32)[:, None] * inv[None, :]
    ang_r = jnp.arange(blk, dtype=F32)[:, None] * inv[None, :]
    ca, sa = jnp.cos(ang_a)[:, None, :], jnp.sin(ang_a)[:, None, :]
    cr, sr = jnp.cos(ang_r)[None, :, :], jnp.sin(ang_r)[None, :, :]
    cos = (ca * cr - sa * sr).reshape(t, half)
    sin = (sa * cr + ca * sr).reshape(t, half)

    proj = _norm_proj(x.reshape(b * t, d), w["norm_mix"][0:1], w["ret_w_in"], tm)
    x, ret_s = _retention(proj.reshape(b, t, -1), x, cos, sin, ret_state, w["ret_gn_g"],
                          w["ret_w_out"], tb, ret_lc)
    ffn = (w["norm_ffn"], w["ffn_w_up"], w["ffn_conv_w"], w["ffn_conv_b"], w["ffn_w_down"],
           conv_state, w["norm_final"])
    x, conv0 = _conv_ffn(x, *ffn, 0, _tile(t, FFN_TILE), False)
    proj = _norm_proj(x.reshape(b * t, d), w["norm_mix"][1:2], w["gla_w_in"], tm)
    x, gla_s = _gla(proj.reshape(b, t, -1), x, gla_state, w["gla_w_a2"], w["gla_b_a"],
                    w["gla_norm_g"], w["gla_w_out"], gla_tb, gla_lc)
    x, conv1 = _conv_ffn(x, *ffn, 1, _tile(t, FFN_TILE), True)
    return x, ret_s[None], gla_s[None], jnp.stack([conv0, conv1])


def kernel(x_prompt, x_sample, state_ret, state_gla, cache_conv, norm_mix, norm_ffn, norm_final,
           ret_w_in, ret_gn_g, ret_w_out, gla_w_in, gla_w_a2, gla_b_a, gla_norm_g, gla_w_out,
           ffn_w_up, ffn_conv_w, ffn_conv_b, ffn_w_down):
    assert norm_mix.shape[0] == 2 and ret_w_in.shape[0] == 1 and gla_w_in.shape[0] == 1
    d = x_prompt.shape[-1]
    gla_qk = gla_w_a2.shape[-1]
    gla_v = gla_w_out.shape[1]
    n_main = 2 * gla_qk + 2 * gla_v
    gla_w_in_p = jnp.concatenate(
        [gla_w_in[0, :, :n_main],
         jnp.pad(gla_w_in[0, :, n_main:], ((0, 0), (0, LANES - GLA_GATE_RANK)))], axis=1)
    w = dict(
        norm_mix=norm_mix, norm_ffn=norm_ffn[:, None, :], norm_final=norm_final.reshape(1, d),
        ret_w_in=ret_w_in[0].astype(BF16),
        ret_gn_g=ret_gn_g[0].reshape(1, -1),
        ret_w_out=ret_w_out[0].astype(BF16),
        gla_w_in=gla_w_in_p.astype(BF16),
        gla_w_a2=jnp.pad(gla_w_a2[0], ((0, LANES - GLA_GATE_RANK), (0, 0))).astype(BF16),
        gla_b_a=gla_b_a[0].reshape(1, -1),
        gla_norm_g=gla_norm_g[0].reshape(1, -1),
        gla_w_out=gla_w_out[0].astype(BF16),
        ffn_w_up=ffn_w_up.astype(BF16), ffn_conv_w=ffn_conv_w, ffn_conv_b=ffn_conv_b[:, None, :],
        ffn_w_down=ffn_w_down.astype(BF16),
    )
    bp = x_prompt.shape[0]
    ret0 = jnp.zeros((bp,) + state_ret.shape[2:], F32)
    gla0 = jnp.zeros((bp,) + state_gla.shape[2:], F32)
    conv0 = jnp.zeros((cache_conv.shape[0], bp) + cache_conv.shape[2:], F32)
    y_p, ret_p, gla_p, conv_p = _trunk(x_prompt, 0, ret0, gla0, conv0, w)
    y_s, ret_s, gla_s, conv_s = _trunk(x_sample, PAST_LEN, state_ret[0], state_gla[0],
                                       cache_conv, w)
    return (y_p, y_s, ret_p, ret_s, gla_p, gla_s, conv_p, conv_s)
```
